```python
import math
import jax, jax.numpy as jnp
from jax import lax
import numpy as np

D_MODEL = 2048
BATCH = 8
SEQ = 4096
DEPTH = 2

N_HEADS = 16
HEAD_DIM = D_MODEL // N_HEADS
ROT_DIM = HEAD_DIM // 4
ROPE_THETA = 500000.0
NSA_KV_GROUPS = 4
NSA_CMP_LEN = 32
NSA_CMP_STRIDE = 16
NSA_CMP_HIDDEN = 2 * HEAD_DIM
NSA_SLC_LEN = 64
NSA_N_SEL = 16
NSA_WINDOW = 512
NSA_QBLOCK = 32
NSA_FORCE_BONUS = 1000.0
FOX_QBLOCK = 128
D_FF_DENSE = 5632
N_EXPERTS = 8
TOP_K = 2
D_FF_EXPERT = 7168
MOE_CHUNK = 128
RMS_EPS = 1e-6
NEG_INF = -1e30
N_NSA_LAYERS = (DEPTH + 1) // 2
N_FOX_LAYERS = DEPTH // 2
NSA_IN_DIM = N_HEADS * HEAD_DIM + 6 * NSA_KV_GROUPS * HEAD_DIM + 3 * N_HEADS
FOX_IN_DIM = 3 * N_HEADS * HEAD_DIM + N_HEADS

kernel_name = "hybrid_nsa_fox_moe_sandwich"


def rms_norm(x, g):
    xf = x.astype(jnp.float32)
    y = xf * lax.rsqrt(jnp.mean(xf * xf, axis=-1, keepdims=True) + RMS_EPS)
    return (y * g.astype(jnp.float32)).astype(x.dtype)


def masked_softmax(s, mask):
    s = jnp.where(mask, s, NEG_INF)
    m = jnp.max(s, axis=-1, keepdims=True)
    e = jnp.where(mask, jnp.exp(s - m), 0.0)
    return e / jnp.maximum(jnp.sum(e, axis=-1, keepdims=True), 1e-30)


def rotary_tables(S):
    pos = jnp.arange(S, dtype=jnp.float32)
    inv = ROPE_THETA ** (-jnp.arange(0, ROT_DIM, 2, dtype=jnp.float32) / ROT_DIM)
    ang = pos[:, None] * inv[None, :]
    return jnp.cos(ang), jnp.sin(ang)


def partial_rotary(x, cos, sin):
    half = ROT_DIM // 2
    c = cos[None, :, None, :].astype(x.dtype)
    s = sin[None, :, None, :].astype(x.dtype)
    x1 = x[..., :half]
    x2 = x[..., half:ROT_DIM]
    return jnp.concatenate([x1 * c - x2 * s, x2 * c + x1 * s, x[..., ROT_DIM:]], axis=-1)


def nsa_mixer(h, w_in, cmp_pos_k, cmp_pos_v, cmp_wk1, cmp_wk2, cmp_wv1, cmp_wv2, w_out):
    B, S, _ = h.shape
    H, G, dh = N_HEADS, NSA_KV_GROUPS, HEAD_DIM
    R = H // G
    L, STRIDE, LS, W, QB = NSA_CMP_LEN, NSA_CMP_STRIDE, NSA_SLC_LEN, NSA_WINDOW, NSA_QBLOCK
    q_dim, kv_dim = H * dh, G * dh
    proj = h @ w_in
    cuts = [int(c) for c in np.cumsum([q_dim] + [kv_dim] * 6)]
    q, kc, vc, ksl, vsl, kw, vw, g = jnp.split(proj, cuts, axis=-1)
    q = q.reshape(B, S, H, dh)
    kc, vc, ksl, vsl, kw, vw = [t.reshape(B, S, G, dh) for t in (kc, vc, ksl, vsl, kw, vw)]
    gates = jax.nn.sigmoid(g.astype(jnp.float32)).astype(h.dtype).reshape(B, S, 3, G, R)
    cos, sin = rotary_tables(S)
    q = partial_rotary(q, cos, sin)
    kc = partial_rotary(kc, cos, sin)
    ksl = partial_rotary(ksl, cos, sin)
    kw = partial_rotary(kw, cos, sin)
    scale = 1.0 / math.sqrt(dh)

    nc = (S - L) // STRIDE + 1
    cmp_start = jnp.arange(nc) * STRIDE
    cmp_end = cmp_start + L - 1
    blk_idx = cmp_start[:, None] + jnp.arange(L)[None, :]

    def compress(t, pos, w1, w2):
        tb = t[:, blk_idx] + pos[:, None, :]
        tb = tb.transpose(0, 1, 3, 2, 4).reshape(B, nc, G, L * dh)
        return jax.nn.gelu(tb @ w1) @ w2

    ck = compress(kc, cmp_pos_k, cmp_wk1, cmp_wk2)
    cv = compress(vc, cmp_pos_v, cmp_wv1, cmp_wv2)

    ns = S // LS
    slc_start = jnp.arange(ns) * LS
    overlap = ((cmp_start[:, None] < slc_start[None, :] + LS)
               & (cmp_end[:, None] >= slc_start[None, :])).astype(jnp.float32)
    k_blocks = ksl.reshape(B, ns, LS, G, dh).transpose(0, 3, 1, 2, 4)
    v_blocks = vsl.reshape(B, ns, LS, G, dh).transpose(0, 3, 1, 2, 4)
    n_sel = min(NSA_N_SEL, ns)
    b_ix = jnp.arange(B)[:, None, None, None]
    g_ix = jnp.arange(G)[None, :, None, None]
    blk_ids = jnp.arange(ns)

    kw_pad = jnp.pad(kw, ((0, 0), (W, 0), (0, 0), (0, 0)))
    vw_pad = jnp.pad(vw, ((0, 0), (W, 0), (0, 0), (0, 0)))

    def block(qb):
        t0 = qb * QB
        qpos = t0 + jnp.arange(QB)
        qblk = lax.dynamic_slice_in_dim(q, t0, QB, axis=1).reshape(B, QB, G, R, dh)
        gblk = lax.dynamic_slice_in_dim(gates, t0, QB, axis=1)

        s = jnp.einsum('bqgrd,bcgd->bgrqc', qblk, ck).astype(jnp.float32) * scale
        p_cmp = masked_softmax(s, cmp_end[None, :] <= qpos[:, None])
        o_cmp = jnp.einsum('bgrqc,bcgd->bqgrd', p_cmp.astype(cv.dtype), cv)

        imp = jnp.einsum('bgrqc,cn->bgqn', p_cmp, overlap)
        jq = (qpos // LS)[:, None]
        forced = (blk_ids == 0) | (blk_ids == jq) | (blk_ids == jq - 1)
        score = jnp.where(blk_ids <= jq, imp + forced * NSA_FORCE_BONUS, -1.0)
        _, sel = lax.top_k(score, n_sel)
        k_sel = k_blocks[b_ix, g_ix, sel].reshape(B, G, QB, n_sel * LS, dh)
        v_sel = v_blocks[b_ix, g_ix, sel].reshape(B, G, QB, n_sel * LS, dh)
        kpos = (sel[..., None] * LS + jnp.arange(LS)).reshape(B, G, QB, n_sel * LS)
        s = jnp.einsum('bqgrd,bgqmd->bgrqm', qblk, k_sel).astype(jnp.float32) * scale
        p = masked_softmax(s, (kpos <= qpos[:, None])[:, :, None])
        o_slc = jnp.einsum('bgrqm,bgqmd->bqgrd', p.astype(v_sel.dtype), v_sel)

        kwb = lax.dynamic_slice_in_dim(kw_pad, t0, QB + W, axis=1)
        vwb = lax.dynamic_slice_in_dim(vw_pad, t0, QB + W, axis=1)
        kpos_w = t0 - W + jnp.arange(QB + W)
        mask_w = ((kpos_w[None, :] <= qpos[:, None]) & (kpos_w[None, :] > qpos[:, None] - W)
                  & (kpos_w[None, :] >= 0))
        s = jnp.einsum('bqgrd,bkgd->bgrqk', qblk, kwb).astype(jnp.float32) * scale
        p = masked_softmax(s, mask_w)
        o_win = jnp.einsum('bgrqk,bkgd->bqgrd', p.astype(vwb.dtype), vwb)

        o = (gblk[:, :, 0, :, :, None] * o_cmp + gblk[:, :, 1, :, :, None] * o_slc
             + gblk[:, :, 2, :, :, None] * o_win)
        return o.reshape(B, QB, H * dh)

    out = lax.map(block, jnp.arange(S // QB))
    out = out.transpose(1, 0, 2, 3).reshape(B, S, H * dh)
    return out @ w_out


def fox_mixer(h, w_in, f_bias, w_out):
    B, S, _ = h.shape
    H, dh, QB = N_HEADS, HEAD_DIM, FOX_QBLOCK
    proj = h @ w_in
    q, k, v, fl = jnp.split(proj, [H * dh, 2 * H * dh, 3 * H * dh], axis=-1)
    q = q.reshape(B, S, H, dh)
    k = k.reshape(B, S, H, dh)
    v = v.reshape(B, S, H, dh)
    log_f = jax.nn.log_sigmoid(fl.astype(jnp.float32) + f_bias.astype(jnp.float32))
    c = jnp.cumsum(log_f, axis=1).transpose(0, 2, 1)
    kpos = jnp.arange(S)
    scale = 1.0 / math.sqrt(dh)

    def block(qb):
        t0 = qb * QB
        qpos = t0 + jnp.arange(QB)
        qblk = lax.dynamic_slice_in_dim(q, t0, QB, axis=1)
        cq = lax.dynamic_slice_in_dim(c, t0, QB, axis=2)
        s = (jnp.einsum('bqhd,bkhd->bhqk', qblk, k).astype(jnp.float32) * scale
             + cq[..., None] - c[:, :, None, :])
        p = masked_softmax(s, kpos[None, :] <= qpos[:, None])
        o = jnp.einsum('bhqk,bkhd->bqhd', p.astype(v.dtype), v)
        return o.reshape(B, QB, H * dh)

    out = lax.map(block, jnp.arange(S // QB))
    out = out.transpose(1, 0, 2, 3).reshape(B, S, H * dh)
    return out @ w_out


def swiglu(h, w_gate, w_up, w_down):
    return (jax.nn.silu(h @ w_gate) * (h @ w_up)) @ w_down


def moe_swiglu(h, w_router, w_gate, w_up, w_down):
    B, S, D = h.shape
    logits = (h @ w_router).astype(jnp.float32)
    top_val, top_idx = lax.top_k(logits, TOP_K)
    top_w = jax.nn.softmax(top_val, axis=-1)
    gate = jnp.sum(jax.nn.one_hot(top_idx, N_EXPERTS, dtype=jnp.float32) * top_w[..., None], axis=-2)
    nch = S // MOE_CHUNK
    xc = h.reshape(B, nch, MOE_CHUNK, D).transpose(1, 0, 2, 3)
    gc = gate.astype(h.dtype).reshape(B, nch, MOE_CHUNK, N_EXPERTS).transpose(1, 0, 2, 3)

    def chunk(args):
        xb, gb = args
        a = jnp.einsum('btd,edf->btef', xb, w_gate)
        u = jnp.einsum('btd,edf->btef', xb, w_up)
        return jnp.einsum('btef,efd->btd', jax.nn.silu(a) * u * gb[..., None], w_down)

    y = lax.map(chunk, (xc, gc))
    return y.transpose(1, 0, 2, 3).reshape(B, S, D)


def setup_inputs(seed: int = 0) -> dict:
    key = jax.random.key(seed)
    ks = jax.random.split(key, 24)
    f32 = jnp.float32
    D, dh = D_MODEL, HEAD_DIM
    na, nf = N_NSA_LAYERS, N_FOX_LAYERS

    def nrm(k, shape, scale):
        return jax.random.normal(k, shape, f32) * scale

    return {
        "x": nrm(ks[0], (BATCH, SEQ, D), 1.0),
        "ln_gains": 1.0 + nrm(ks[1], (DEPTH, 4, D), 0.05),
        "nsa_w_in": nrm(ks[2], (na, D, NSA_IN_DIM), D ** -0.5),
        "nsa_cmp_pos_k": nrm(ks[3], (na, NSA_CMP_LEN, dh), 0.1),
        "nsa_cmp_pos_v": nrm(ks[4], (na, NSA_CMP_LEN, dh), 0.1),
        "nsa_cmp_wk1": nrm(ks[5], (na, NSA_CMP_LEN * dh, NSA_CMP_HIDDEN), (NSA_CMP_LEN * dh) ** -0.5),
        "nsa_cmp_wk2": nrm(ks[6], (na, NSA_CMP_HIDDEN, dh), NSA_CMP_HIDDEN ** -0.5),
        "nsa_cmp_wv1": nrm(ks[7], (na, NSA_CMP_LEN * dh, NSA_CMP_HIDDEN), (NSA_CMP_LEN * dh) ** -0.5),
        "nsa_cmp_wv2": nrm(ks[8], (na, NSA_CMP_HIDDEN, dh), NSA_CMP_HIDDEN ** -0.5),
        "nsa_w_out": nrm(ks[9], (na, N_HEADS * dh, D), (N_HEADS * dh) ** -0.5),
        "fox_w_in": nrm(ks[10], (nf, D, FOX_IN_DIM), D ** -0.5),
        "fox_f_bias": jax.random.uniform(ks[11], (nf, N_HEADS), f32, 1.0, 4.0),
        "fox_w_out": nrm(ks[12], (nf, N_HEADS * dh, D), (N_HEADS * dh) ** -0.5),
        "ffn_w_gate": nrm(ks[13], (na, D, D_FF_DENSE), D ** -0.5),
        "ffn_w_up": nrm(ks[14], (na, D, D_FF_DENSE), D ** -0.5),
        "ffn_w_down": nrm(ks[15], (na, D_FF_DENSE, D), D_FF_DENSE ** -0.5),
        "moe_router": nrm(ks[16], (nf, D, N_EXPERTS), D ** -0.5),
        "moe_w_gate": nrm(ks[17], (nf, N_EXPERTS, D, D_FF_EXPERT), D ** -0.5),
        "moe_w_up": nrm(ks[18], (nf, N_EXPERTS, D, D_FF_EXPERT), D ** -0.5),
        "moe_w_down": nrm(ks[19], (nf, N_EXPERTS, D_FF_EXPERT, D), D_FF_EXPERT ** -0.5),
    }


def reference(x, ln_gains, nsa_w_in, nsa_cmp_pos_k, nsa_cmp_pos_v, nsa_cmp_wk1, nsa_cmp_wk2,
              nsa_cmp_wv1, nsa_cmp_wv2, nsa_w_out, fox_w_in, fox_f_bias, fox_w_out,
              ffn_w_gate, ffn_w_up, ffn_w_down, moe_router, moe_w_gate, moe_w_up, moe_w_down):
    for i in range(DEPTH):
        j = i // 2
        hn = rms_norm(x, ln_gains[i, 0])
        if i % 2 == 0:
            m = nsa_mixer(hn, nsa_w_in[j], nsa_cmp_pos_k[j], nsa_cmp_pos_v[j], nsa_cmp_wk1[j],
                          nsa_cmp_wk2[j], nsa_cmp_wv1[j], nsa_cmp_wv2[j], nsa_w_out[j])
        else:
            m = fox_mixer(hn, fox_w_in[j], fox_f_bias[j], fox_w_out[j])
        x = x + rms_norm(m, ln_gains[i, 1])
        hn = rms_norm(x, ln_gains[i, 2])
        if i % 2 == 0:
            f = swiglu(hn, ffn_w_gate[j], ffn_w_up[j], ffn_w_down[j])
        else:
            f = moe_swiglu(hn, moe_router[j], moe_w_gate[j], moe_w_up[j], moe_w_down[j])
        x = x + rms_norm(f, ln_gains[i, 3])
    return x
```

```python
import functools
import math

import jax
import jax.numpy as jnp
from jax import lax
from jax.experimental import pallas as pl
from jax.experimental.pallas import tpu as pltpu

N_HEADS = 16
HEAD_DIM = 128
ROT_DIM = HEAD_DIM // 4
ROPE_THETA = 500000.0
KV_GROUPS = 4
HEADS_PER_GROUP = N_HEADS // KV_GROUPS
CMP_LEN = 32
CMP_STRIDE = 16
SLC_LEN = 64
N_SEL = 16
WINDOW = 512
FORCE_BONUS = 1000.0
N_EXPERTS = 8
RMS_EPS = 1e-6
NEG_INF = -1e30
LANES = 128
MXU_DTYPE = jnp.bfloat16
VMEM_LIMIT = 56 * 1024 * 1024
F32 = jnp.float32

_NT = (((1,), (1,)), ((), ()))


def _cparams(sem):
    return pltpu.CompilerParams(dimension_semantics=sem, vmem_limit_bytes=VMEM_LIMIT)


def _rms(x, g):
    ms = jnp.mean(x * x, axis=-1, keepdims=True)
    return x * lax.rsqrt(ms + RMS_EPS) * g


def _split3(x):
    a = x.astype(MXU_DTYPE)
    r = x - a.astype(F32)
    b = r.astype(MXU_DTYPE)
    c = (r - b.astype(F32)).astype(MXU_DTYPE)
    return a, b, c


def _rmsnorm_kernel(x_ref, g_ref, o_ref):
    o_ref[...] = _rms(x_ref[...], g_ref[...]).astype(o_ref.dtype)


def _rmsnorm(x, g, tm=512):
    m, d = x.shape
    return pl.pallas_call(
        _rmsnorm_kernel,
        out_shape=jax.ShapeDtypeStruct((m, d), MXU_DTYPE),
        grid=(m // tm,),
        in_specs=[pl.BlockSpec((tm, d), lambda i: (i, 0)), pl.BlockSpec((1, d), lambda i: (0, 0))],
        out_specs=pl.BlockSpec((tm, d), lambda i: (i, 0)),
        compiler_params=_cparams(("parallel",)),
        name="rmsnorm",
    )(x, g)


def _any_tile(j, tiles):
    return functools.reduce(jnp.logical_or, [j == t for t in tiles])


def _proj_rot_kernel(a_ref, w_ref, *rest, rot_tiles, scale_tiles, scale):
    o_ref = rest[-1]
    j = pl.program_id(1)
    out = jnp.dot(a_ref[...], w_ref[...], preferred_element_type=F32)
    tn = out.shape[1]
    if rot_tiles:
        c_ref, s1_ref, s2_ref = rest[:3]
        rotf = jnp.where(_any_tile(j, rot_tiles), 1.0, 0.0).astype(F32)
        reps = tn // HEAD_DIM
        c = jnp.concatenate([1.0 + rotf * (c_ref[...] - 1.0)] * reps, axis=1)
        s1 = jnp.concatenate([rotf * s1_ref[...]] * reps, axis=1)
        s2 = jnp.concatenate([rotf * s2_ref[...]] * reps, axis=1)
        half = ROT_DIM // 2
        out = out * c + pltpu.roll(out, half, 1) * s1 + pltpu.roll(out, tn - half, 1) * s2
    if scale_tiles:
        out = out * jnp.where(_any_tile(j, scale_tiles), scale, 1.0).astype(F32)
    o_ref[...] = out.astype(o_ref.dtype)


def _proj_rot(a, w, tabs, seq, *, rot_tiles, scale_tiles, scale, out_dtype, tm=512, tn=512):
    m, k = a.shape
    n = w.shape[1]
    nrow = seq // tm
    tab_spec = pl.BlockSpec((tm, HEAD_DIM), lambda i, j: (i % nrow, 0))
    tabs = tuple(tabs) if rot_tiles else ()
    return pl.pallas_call(
        functools.partial(_proj_rot_kernel, rot_tiles=rot_tiles, scale_tiles=scale_tiles, scale=scale),
        out_shape=jax.ShapeDtypeStruct((m, n), out_dtype),
        grid=(m // tm, n // tn),
        in_specs=[pl.BlockSpec((tm, k), lambda i, j: (i, 0)),
                  pl.BlockSpec((k, tn), lambda i, j: (0, j))] + [tab_spec] * len(tabs),
        out_specs=pl.BlockSpec((tm, tn), lambda i, j: (i, j)),
        compiler_params=_cparams(("parallel", "arbitrary")),
        name="proj_rot" if rot_tiles else "proj_plain",
    )(a, w, *tabs)


def _proj_act_kernel(a_ref, w_ref, b_ref, o_ref, *, act):
    acc = jnp.dot(a_ref[...], w_ref[...], preferred_element_type=F32) + b_ref[...]
    if act == "sigmoid":
        acc = jax.nn.sigmoid(acc)
    elif act == "log_sigmoid":
        acc = jax.nn.log_sigmoid(acc)
    o_ref[...] = acc.astype(o_ref.dtype)


def _proj_act(a, w, b, *, act, out_dtype, tm=512, tn=512):
    m, k = a.shape
    n = w.shape[1]
    tn = min(tn, n)
    return pl.pallas_call(
        functools.partial(_proj_act_kernel, act=act),
        out_shape=jax.ShapeDtypeStruct((m, n), out_dtype),
        grid=(m // tm, n // tn),
        in_specs=[pl.BlockSpec((tm, k), lambda i, j: (i, 0)),
                  pl.BlockSpec((k, tn), lambda i, j: (0, j)),
                  pl.BlockSpec((1, tn), lambda i, j: (0, j))],
        out_specs=pl.BlockSpec((tm, tn), lambda i, j: (i, j)),
        compiler_params=_cparams(("parallel", "arbitrary")),
        name="proj_act",
    )(a, w, b)


def _compress_kernel(x_ref, posk_ref, posv_ref, wka_ref, wkb_ref, wva_ref, wvb_ref, w2k_ref, w2v_ref,
                     ck_ref, cv_ref, acc_ref):
    l = pl.program_id(1)
    nl = pl.num_programs(1)

    @pl.when(l == 0)
    def _():
        acc_ref[...] = jnp.zeros_like(acc_ref)

    x = x_ref[0]
    pos = (posk_ref, posv_ref)
    wa = (wka_ref, wva_ref)
    wb = (wkb_ref, wvb_ref)
    for kv in range(2):
        pa = pos[kv][pl.ds(l, 1), :]
        pb = pos[kv][pl.ds(l + CMP_STRIDE, 1), :]
        for g in range(KV_GROUPS):
            off = (kv * KV_GROUPS + g) * HEAD_DIM
            xg = x[:, off:off + HEAD_DIM]
            acc_ref[kv, g, 0] += jnp.dot((xg + pa).astype(MXU_DTYPE), wa[kv][...], preferred_element_type=F32)
            acc_ref[kv, g, 1] += jnp.dot((xg + pb).astype(MXU_DTYPE), wb[kv][...], preferred_element_type=F32)

    @pl.when(l == nl - 1)
    def _():
        rows = acc_ref.shape[3]
        w2 = (w2k_ref, w2v_ref)
        outs = (ck_ref, cv_ref)
        for kv in range(2):
            for g in range(KV_GROUPS):
                nxt = pltpu.roll(acc_ref[kv, g, 1], rows - 1, 0)
                h = jax.nn.gelu(acc_ref[kv, g, 0] + nxt)
                outs[kv][0, g] = jnp.dot(h.astype(MXU_DTYPE), w2[kv][...],
                                         preferred_element_type=F32).astype(outs[kv].dtype)


def _compress(kcv, pos_k, pos_v, w1k, w1v, w2k, w2v, batch, seq):
    chunks = seq // CMP_STRIDE
    width = 2 * KV_GROUPS * HEAD_DIM
    x = kcv.reshape(batch, chunks, CMP_STRIDE * width)
    hid = w1k.shape[1]
    wa_spec = pl.BlockSpec((HEAD_DIM, hid), lambda b, l: (l, 0))
    wb_spec = pl.BlockSpec((HEAD_DIM, hid), lambda b, l: (l + CMP_STRIDE, 0))
    full = lambda shp: pl.BlockSpec(shp, lambda b, l: (0,) * len(shp))
    out_sds = jax.ShapeDtypeStruct((batch, KV_GROUPS, chunks, HEAD_DIM), MXU_DTYPE)
    out_spec = pl.BlockSpec((1, KV_GROUPS, chunks, HEAD_DIM), lambda b, l: (b, 0, 0, 0))
    return pl.pallas_call(
        _compress_kernel,
        out_shape=(out_sds, out_sds),
        grid=(batch, CMP_STRIDE),
        in_specs=[pl.BlockSpec((1, chunks, width), lambda b, l: (b, 0, l)),
                  full(pos_k.shape), full(pos_v.shape),
                  wa_spec, wb_spec, wa_spec, wb_spec,
                  full(w2k.shape), full(w2v.shape)],
        out_specs=(out_spec, out_spec),
        scratch_shapes=[pltpu.VMEM((2, KV_GROUPS, 2, chunks, hid), F32)],
        compiler_params=_cparams(("parallel", "arbitrary")),
        name="nsa_compress",
    )(x, pos_k, pos_v, w1k, w1k, w1v, w1v, w2k, w2v)


def _softmax_update(s, v, m_scr, l_scr, acc_scr):
    m_prev = m_scr[...]
    m_new = jnp.maximum(m_prev, jnp.max(s, axis=1, keepdims=True))
    alpha = jnp.exp(m_prev - m_new)
    p = jnp.exp(s - m_new)
    l_scr[...] = alpha * l_scr[...] + jnp.sum(p, axis=1, keepdims=True)
    acc_scr[...] = alpha * acc_scr[...] + jnp.dot(p.astype(MXU_DTYPE), v, preferred_element_type=F32)
    m_scr[...] = m_new


def _nsa_attn_kernel(q_ref, ksl_ref, vsl_ref, kw_ref, vw_ref, ck_ref, cv_ref, gate_ref, ovl_ref, exp_ref,
                     o_ref, m_scr, l_scr, acc_scr, *, tq, tk, ns, n_sel):
    g = pl.program_id(1)
    qi = pl.program_id(2)
    q0 = qi * tq
    rph = HEADS_PER_GROUP
    rows = rph * tq
    qb = q_ref[...]
    qrows = jnp.concatenate([qb[:, r * HEAD_DIM:(r + 1) * HEAD_DIM] for r in range(rph)], axis=0)
    qpos = q0 + lax.broadcasted_iota(jnp.int32, (tq, 1), 0)

    ck = ck_ref[0, 0]
    ncp = ck.shape[0]
    s = lax.dot_general(qrows, ck, _NT, preferred_element_type=F32).reshape(rph, tq, ncp)
    cidx = lax.broadcasted_iota(jnp.int32, (tq, ncp), 1)
    valid = (cidx * CMP_STRIDE + (CMP_LEN - 1) <= qpos)[None]
    s = jnp.where(valid, s, NEG_INF)
    e = jnp.where(valid, jnp.exp(s - jnp.max(s, axis=-1, keepdims=True)), 0.0)
    p = e / jnp.maximum(jnp.sum(e, axis=-1, keepdims=True), 1e-30)
    o_cmp = jnp.dot(p.reshape(rows, ncp).astype(MXU_DTYPE), cv_ref[0, 0], preferred_element_type=F32)

    psum = p[0]
    for r in range(1, rph):
        psum = psum + p[r]
    hi = psum.astype(MXU_DTYPE)
    lo = (psum - hi.astype(F32)).astype(MXU_DTYPE)
    ovl = ovl_ref[...]
    imp = jnp.dot(hi, ovl, preferred_element_type=F32) + jnp.dot(lo, ovl, preferred_element_type=F32)
    nidx = lax.broadcasted_iota(jnp.int32, (tq, LANES), 1)
    jq = lax.shift_right_logical(qpos, int(math.log2(SLC_LEN)))
    forced = (nidx == 0) | (nidx == jq) | (nidx == jq - 1)
    score = jnp.where(nidx <= jq, imp + jnp.where(forced, FORCE_BONUS, 0.0), -1.0)
    score = jnp.where(nidx < ns, score, -2.0)
    s_t = score.T
    nsp = ((ns + 7) // 8) * 8
    cand = s_t[:nsp]
    nio = lax.broadcasted_iota(jnp.int32, (nsp, tq), 0)
    cnt = jnp.zeros((nsp, tq), jnp.int32)
    for mm in range(ns):
        row = s_t[mm:mm + 1, :]
        beats = (row > cand) | ((row == cand) & (nio > mm))
        cnt = cnt + jnp.where(beats, 1, 0)
    sel_t = jnp.where(cnt < n_sel, 1.0, 0.0).astype(F32)
    if nsp < LANES:
        sel_t = jnp.concatenate([sel_t, jnp.zeros((LANES - nsp, tq), F32)], axis=0)
    sel = sel_t.T.astype(MXU_DTYPE)

    m_scr[...] = jnp.full_like(m_scr, NEG_INF)
    l_scr[...] = jnp.zeros_like(l_scr)
    acc_scr[...] = jnp.zeros_like(acc_scr)

    def slc_body(kt, carry):
        k0 = pl.multiple_of(kt * tk, tk)
        kb = ksl_ref[pl.ds(k0, tk), :]
        vb = vsl_ref[pl.ds(k0, tk), :]
        sc = lax.dot_general(qrows, kb, _NT, preferred_element_type=F32).reshape(rph, tq, tk)
        selx = jnp.dot(sel, exp_ref[kt], preferred_element_type=F32)
        kpos = k0 + lax.broadcasted_iota(jnp.int32, (tq, tk), 1)
        bias = jnp.where((selx > 0.5) & (kpos <= qpos), 0.0, NEG_INF)
        sc = (sc + bias[None]).reshape(rows, tk)
        _softmax_update(sc, vb, m_scr, l_scr, acc_scr)
        return carry

    lax.fori_loop(0, (q0 + tq + tk - 1) // tk, slc_body, 0)
    o_slc = acc_scr[...] / jnp.maximum(l_scr[...], 1e-30)

    wlen = tq + WINDOW
    st = pl.multiple_of(jnp.maximum(q0 - WINDOW, 0), tq)
    kwb = kw_ref[pl.ds(st, wlen), :]
    vwb = vw_ref[pl.ds(st, wlen), :]
    sw = lax.dot_general(qrows, kwb, _NT, preferred_element_type=F32).reshape(rph, tq, wlen)
    kpos = st + lax.broadcasted_iota(jnp.int32, (tq, wlen), 1)
    okw = (kpos <= qpos) & (kpos > qpos - WINDOW)
    sw = sw + jnp.where(okw, 0.0, NEG_INF)[None]
    pw = jnp.exp(sw - jnp.max(sw, axis=-1, keepdims=True))
    lw = jnp.maximum(jnp.sum(pw, axis=-1, keepdims=True), 1e-30).reshape(rows, 1)
    o_win = jnp.dot(pw.reshape(rows, wlen).astype(MXU_DTYPE), vwb, preferred_element_type=F32) / lw

    gt = gate_ref[...]
    outs = []
    for r in range(rph):
        sl = slice(r * tq, (r + 1) * tq)
        acc = None
        for br, ob in enumerate((o_cmp, o_slc, o_win)):
            lane_id = br * N_HEADS + g * rph + r
            gcol = jnp.sum(jnp.where(nidx == lane_id, gt, 0.0), axis=1, keepdims=True)
            term = gcol * ob[sl]
            acc = term if acc is None else acc + term
        outs.append(acc)
    o_ref[...] = jnp.concatenate(outs, axis=1).astype(o_ref.dtype)


def _nsa_attention(qkv, ck, cv, gates, batch, seq, tq=128, tk=512):
    t = batch * seq
    nq = seq // tq
    ns = seq // SLC_LEN
    n_sel = min(N_SEL, ns)
    ncp = ck.shape[2]
    gw = HEADS_PER_GROUP * HEAD_DIM
    c = jnp.arange(ncp)
    n = jnp.arange(LANES)
    cs, ce = c * CMP_STRIDE, c * CMP_STRIDE + CMP_LEN - 1
    nc = (seq - CMP_LEN) // CMP_STRIDE + 1
    ovl = ((cs[:, None] < n[None, :] * SLC_LEN + SLC_LEN) & (ce[:, None] >= n[None, :] * SLC_LEN)
           & (c[:, None] < nc) & (n[None, :] < ns)).astype(MXU_DTYPE)
    kp = jnp.arange(seq).reshape(seq // tk, 1, tk)
    expand = (kp // SLC_LEN == n[None, :, None]).astype(MXU_DTYPE)
    kv_col0 = N_HEADS

    def kv_spec(which):
        return pl.BlockSpec((seq, HEAD_DIM), lambda b, g, i: (b, kv_col0 + which * KV_GROUPS + g))

    cmp_spec = pl.BlockSpec((1, 1, ncp, HEAD_DIM), lambda b, g, i: (b, g, 0, 0))
    rows = HEADS_PER_GROUP * tq
    return pl.pallas_call(
        functools.partial(_nsa_attn_kernel, tq=tq, tk=tk, ns=ns, n_sel=n_sel),
        out_shape=jax.ShapeDtypeStruct((t, N_HEADS * HEAD_DIM), MXU_DTYPE),
        grid=(batch, KV_GROUPS, nq),
        in_specs=[pl.BlockSpec((tq, gw), lambda b, g, i: (b * nq + i, g)),
                  kv_spec(0), kv_spec(1), kv_spec(2), kv_spec(3),
                  cmp_spec, cmp_spec,
                  pl.BlockSpec((tq, LANES), lambda b, g, i: (b * nq + i, 0)),
                  pl.BlockSpec(ovl.shape, lambda b, g, i: (0, 0)),
                  pl.BlockSpec(expand.shape, lambda b, g, i: (0, 0, 0))],
        out_specs=pl.BlockSpec((tq, gw), lambda b, g, i: (b * nq + i, g)),
        scratch_shapes=[pltpu.VMEM((rows, 1), F32), pltpu.VMEM((rows, 1), F32), pltpu.VMEM((rows, HEAD_DIM), F32)],
        compiler_params=_cparams(("parallel", "parallel", "arbitrary")),
        name="nsa_attention",
    )(qkv, qkv, qkv, qkv, qkv, ck, cv, gates, ovl, expand)


def _cumsum_kernel(x_ref, o_ref, carry_ref):
    @pl.when(pl.program_id(1) == 0)
    def _():
        carry_ref[...] = jnp.zeros_like(carry_ref)

    x = x_ref[...]
    n = x.shape[0]
    tri = (lax.broadcasted_iota(jnp.int32, (n, n), 1) <= lax.broadcasted_iota(jnp.int32, (n, n), 0))
    tri = jnp.where(tri, 1.0, 0.0).astype(MXU_DTYPE)
    a, b, c = _split3(x)
    inc = (jnp.dot(tri, a, preferred_element_type=F32) + jnp.dot(tri, b, preferred_element_type=F32)
           + jnp.dot(tri, c, preferred_element_type=F32))
    out = inc + carry_ref[...]
    o_ref[...] = out
    carry_ref[...] = out[n - 1:n, :]


def _cumsum_rows(x, batch, seq, tb=512):
    nb = seq // tb
    return pl.pallas_call(
        _cumsum_kernel,
        out_shape=jax.ShapeDtypeStruct(x.shape, F32),
        grid=(batch, nb),
        in_specs=[pl.BlockSpec((tb, x.shape[1]), lambda b, i: (b * nb + i, 0))],
        out_specs=pl.BlockSpec((tb, x.shape[1]), lambda b, i: (b * nb + i, 0)),
        scratch_shapes=[pltpu.VMEM((1, x.shape[1]), F32)],
        compiler_params=_cparams(("parallel", "arbitrary")),
        name="fox_cumsum",
    )(x)


def _fox_attn_kernel(q_ref, k_ref, v_ref, cq_ref, ck_ref, o_ref, m_scr, l_scr, acc_scr, *, tq):
    h = pl.program_id(1)
    qi = pl.program_id(2)
    q = q_ref[...]
    lane = lax.broadcasted_iota(jnp.int32, (tq, LANES), 1)
    cq = jnp.sum(jnp.where(lane == h, cq_ref[...], 0.0), axis=1, keepdims=True)
    m_scr[...] = jnp.full_like(m_scr, NEG_INF)
    l_scr[...] = jnp.zeros_like(l_scr)
    acc_scr[...] = jnp.zeros_like(acc_scr)

    def scores(kt):
        k0 = pl.multiple_of(kt * tq, tq)
        s = lax.dot_general(q, k_ref[pl.ds(k0, tq), :], _NT, preferred_element_type=F32)
        return s + cq - ck_ref[0, 0, pl.ds(kt, 1), :], v_ref[pl.ds(k0, tq), :]

    def body(kt, carry):
        s, vb = scores(kt)
        _softmax_update(s, vb, m_scr, l_scr, acc_scr)
        return carry

    lax.fori_loop(0, qi, body, 0)
    s, vb = scores(qi)
    causal = lax.broadcasted_iota(jnp.int32, (tq, tq), 1) <= lax.broadcasted_iota(jnp.int32, (tq, tq), 0)
    _softmax_update(jnp.where(causal, s, NEG_INF), vb, m_scr, l_scr, acc_scr)
    o_ref[...] = (acc_scr[...] / jnp.maximum(l_scr[...], 1e-30)).astype(o_ref.dtype)


def _fox_attention(qkv, c_tok, c_head, batch, seq, tq=512):
    t = batch * seq
    nq = seq // tq
    return pl.pallas_call(
        functools.partial(_fox_attn_kernel, tq=tq),
        out_shape=jax.ShapeDtypeStruct((t, N_HEADS * HEAD_DIM), MXU_DTYPE),
        grid=(batch, N_HEADS, nq),
        in_specs=[pl.BlockSpec((tq, HEAD_DIM), lambda b, h, i: (b * nq + i, h)),
                  pl.BlockSpec((seq, HEAD_DIM), lambda b, h, i: (b, N_HEADS + h)),
                  pl.BlockSpec((seq, HEAD_DIM), lambda b, h, i: (b, 2 * N_HEADS + h)),
                  pl.BlockSpec((tq, LANES), lambda b, h, i: (b * nq + i, 0)),
                  pl.BlockSpec((1, 1, nq, tq), lambda b, h, i: (b, h, 0, 0))],
        out_specs=pl.BlockSpec((tq, HEAD_DIM), lambda b, h, i: (b * nq + i, h)),
        scratch_shapes=[pltpu.VMEM((tq, 1), F32), pltpu.VMEM((tq, 1), F32), pltpu.VMEM((tq, HEAD_DIM), F32)],
        compiler_params=_cparams(("parallel", "parallel", "arbitrary")),
        name="fox_attention",
    )(qkv, qkv, qkv, c_tok, c_head)


def _outproj_kernel(a_ref, w_ref, x_ref, g1_ref, *rest):
    y = jnp.dot(a_ref[...], w_ref[...], preferred_element_type=F32)
    xn = x_ref[...] + _rms(y, g1_ref[...])
    if len(rest) == 1:
        rest[0][...] = xn
    else:
        g2_ref, xo_ref, hn_ref = rest
        xo_ref[...] = xn
        hn_ref[...] = _rms(xn, g2_ref[...]).astype(hn_ref.dtype)


def _outproj(a, w, x, g1, g2=None, tm=256):
    m, k = a.shape
    d = w.shape[1]
    row = lambda i: (i, 0)
    fix = lambda i: (0, 0)
    in_specs = [pl.BlockSpec((tm, k), row), pl.BlockSpec((k, d), fix), pl.BlockSpec((tm, d), row),
                pl.BlockSpec((1, d), fix)]
    x_sds, x_spec = jax.ShapeDtypeStruct((m, d), F32), pl.BlockSpec((tm, d), row)
    if g2 is None:
        args, out_shape, out_specs = (a, w, x, g1), x_sds, x_spec
    else:
        args = (a, w, x, g1, g2)
        in_specs = in_specs + [pl.BlockSpec((1, d), fix)]
        out_shape = (x_sds, jax.ShapeDtypeStruct((m, d), MXU_DTYPE))
        out_specs = (x_spec, pl.BlockSpec((tm, d), row))
    return pl.pallas_call(
        _outproj_kernel,
        out_shape=out_shape,
        grid=(m // tm,),
        in_specs=in_specs,
        out_specs=out_specs,
        compiler_params=_cparams(("parallel",)),
        name="outproj_norm",
    )(*args)


def _ffn_kernel(h_ref, wg_ref, wu_ref, wd_ref, x_ref, g1_ref, g2_ref, xo_ref, hn_ref, acc_ref):
    f = pl.program_id(1)

    @pl.when(f == 0)
    def _():
        acc_ref[...] = jnp.zeros_like(acc_ref)

    h = h_ref[...]
    a = jnp.dot(h, wg_ref[...], preferred_element_type=F32)
    u = jnp.dot(h, wu_ref[...], preferred_element_type=F32)
    acc_ref[...] += jnp.dot((jax.nn.silu(a) * u).astype(MXU_DTYPE), wd_ref[...], preferred_element_type=F32)

    @pl.when(f == pl.num_programs(1) - 1)
    def _():
        xn = x_ref[...] + _rms(acc_ref[...], g1_ref[...])
        xo_ref[...] = xn
        hn_ref[...] = _rms(xn, g2_ref[...]).astype(hn_ref.dtype)


def _ffn(h, wg, wu, wd, x, g1, g2, tm=512, tf=512):
    m, d = h.shape
    ff = wg.shape[1]
    row = lambda i, f: (i, 0)
    fix = lambda i, f: (0, 0)
    return pl.pallas_call(
        _ffn_kernel,
        out_shape=(jax.ShapeDtypeStruct((m, d), F32), jax.ShapeDtypeStruct((m, d), MXU_DTYPE)),
        grid=(m // tm, ff // tf),
        in_specs=[pl.BlockSpec((tm, d), row),
                  pl.BlockSpec((d, tf), lambda i, f: (0, f)),
                  pl.BlockSpec((d, tf), lambda i, f: (0, f)),
                  pl.BlockSpec((tf, d), lambda i, f: (f, 0)),
                  pl.BlockSpec((tm, d), row), pl.BlockSpec((1, d), fix), pl.BlockSpec((1, d), fix)],
        out_specs=(pl.BlockSpec((tm, d), row), pl.BlockSpec((tm, d), row)),
        scratch_shapes=[pltpu.VMEM((tm, d), F32)],
        compiler_params=_cparams(("parallel", "arbitrary")),
        name="ffn_swiglu",
    )(h, wg, wu, wd, x, g1, g2)


def _router_kernel(x_ref, g_ref, wr_ref, wts_ref, meta_ref, cnt_ref, carry_ref):
    @pl.when(pl.program_id(0) == 0)
    def _():
        carry_ref[...] = jnp.zeros_like(carry_ref)

    hn = _rms(x_ref[...], g_ref[...])
    tm = hn.shape[0]
    logits = jnp.dot(hn, wr_ref[...], preferred_element_type=F32, precision=lax.Precision.HIGHEST)
    lane = lax.broadcasted_iota(jnp.int32, (tm, LANES), 1)
    lanef = lane.astype(F32)
    lg = jnp.where(lane < N_EXPERTS, logits, -jnp.inf)
    v1 = jnp.max(lg, axis=1, keepdims=True)
    i1 = jnp.min(jnp.where(lg == v1, lanef, float(LANES)), axis=1, keepdims=True)
    lg2 = jnp.where(lanef == i1, -jnp.inf, lg)
    v2 = jnp.max(lg2, axis=1, keepdims=True)
    i2 = jnp.min(jnp.where(lg2 == v2, lanef, float(LANES)), axis=1, keepdims=True)
    e2 = jnp.exp(v2 - v1)
    den = 1.0 + e2
    wts_ref[...] = jnp.where(lane == 0, 1.0 / den, jnp.where(lane == 1, e2 / den, 0.0))

    onehot = jnp.where((lanef == i1) | (lanef == i2), 1.0, 0.0)
    tri = (lax.broadcasted_iota(jnp.int32, (tm, tm), 1) <= lax.broadcasted_iota(jnp.int32, (tm, tm), 0))
    incl = jnp.dot(jnp.where(tri, 1.0, 0.0).astype(MXU_DTYPE), onehot.astype(MXU_DTYPE),
                   preferred_element_type=F32)
    rank = carry_ref[...] + incl - onehot
    r1 = jnp.sum(jnp.where(lanef == i1, rank, 0.0), axis=1, keepdims=True)
    r2 = jnp.sum(jnp.where(lanef == i2, rank, 0.0), axis=1, keepdims=True)
    meta = jnp.where(lane == 0, i1, jnp.where(lane == 1, i2, jnp.where(lane == 2, r1,
                     jnp.where(lane == 3, r2, 0.0))))
    meta_ref[...] = meta.astype(jnp.int32)
    total = carry_ref[...] + incl[tm - 1:tm, :]
    carry_ref[...] = total
    cnt_ref[...] = total


def _router(x, g, wr, tm=512):
    m, d = x.shape
    row = lambda i: (i, 0)
    fix = lambda i: (0, 0)
    return pl.pallas_call(
        _router_kernel,
        out_shape=(jax.ShapeDtypeStruct((m, LANES), F32), jax.ShapeDtypeStruct((m, LANES), jnp.int32),
                   jax.ShapeDtypeStruct((1, LANES), F32)),
        grid=(m // tm,),
        in_specs=[pl.BlockSpec((tm, d), row), pl.BlockSpec((1, d), fix), pl.BlockSpec((d, LANES), fix)],
        out_specs=(pl.BlockSpec((tm, LANES), row), pl.BlockSpec((tm, LANES), row), pl.BlockSpec((1, LANES), fix)),
        scratch_shapes=[pltpu.VMEM((1, LANES), F32)],
        compiler_params=_cparams(("arbitrary",)),
        name="moe_router",
    )(x, g, wr)


def _row_copy(src_ref, s, dst_ref, d, sem):
    return pltpu.make_async_copy(src_ref.at[pl.ds(s, 1)], dst_ref.at[pl.ds(d, 1)], sem)


def _dispatch_kernel(dest_ref, x_ref, zeros_ref, xs_ref, sem, *, tc):
    del zeros_ref
    base = pl.program_id(0) * tc

    def issue(t, carry):
        _row_copy(x_ref, base + t, xs_ref, dest_ref[0, 0, 2 * t], sem).start()
        _row_copy(x_ref, base + t, xs_ref, dest_ref[0, 0, 2 * t + 1], sem).start()
        return carry

    lax.fori_loop(0, tc, issue, 0)

    def drain(t, carry):
        _row_copy(x_ref, 0, xs_ref, 0, sem).wait()
        _row_copy(x_ref, 0, xs_ref, 0, sem).wait()
        return carry

    lax.fori_loop(0, tc, drain, 0)


def _dispatch(x, dest, rows_total, tc=1024):
    m, d = x.shape
    zeros = jnp.zeros((rows_total, d), x.dtype)
    return pl.pallas_call(
        functools.partial(_dispatch_kernel, tc=tc),
        out_shape=jax.ShapeDtypeStruct((rows_total, d), x.dtype),
        grid=(m // tc,),
        in_specs=[pl.BlockSpec((1, 1, 2 * tc), lambda i: (i, 0, 0), memory_space=pltpu.SMEM),
                  pl.BlockSpec(memory_space=pl.ANY), pl.BlockSpec(memory_space=pl.ANY)],
        out_specs=pl.BlockSpec(memory_space=pl.ANY),
        scratch_shapes=[pltpu.SemaphoreType.DMA(())],
        input_output_aliases={2: 0},
        compiler_params=_cparams(("arbitrary",)),
        name="moe_dispatch",
    )(dest.reshape(m // tc, 1, 2 * tc), x, zeros)


def _experts_kernel(te_ref, tv_ref, xs_ref, g_ref, wg_ref, wu_ref, wd_ref, o_ref, hn_ref, acc_ref):
    n = pl.program_id(0)
    f = pl.program_id(1)
    live = tv_ref[n] > 0

    @pl.when(live & (f == 0))
    def _():
        hn_ref[...] = _rms(xs_ref[...], g_ref[...]).astype(hn_ref.dtype)
        acc_ref[...] = jnp.zeros_like(acc_ref)

    @pl.when(live)
    def _():
        h = hn_ref[...]
        a = jnp.dot(h, wg_ref[0], preferred_element_type=F32)
        u = jnp.dot(h, wu_ref[0], preferred_element_type=F32)
        acc_ref[...] += jnp.dot((jax.nn.silu(a) * u).astype(MXU_DTYPE), wd_ref[0], preferred_element_type=F32)

    last = f == pl.num_programs(1) - 1

    @pl.when(live & last)
    def _():
        o_ref[...] = acc_ref[...]

    @pl.when(jnp.logical_not(live) & last)
    def _():
        o_ref[...] = jnp.zeros_like(o_ref)


def _experts(xs, g, wg, wu, wd, tile_expert, tile_valid, tm, tf=512):
    rows, d = xs.shape
    ff = wg.shape[2]
    nf = ff // tf

    def f_eff(n, f, tv):
        return jnp.where(tv[n] > 0, f, nf - 1)

    grid_spec = pltpu.PrefetchScalarGridSpec(
        num_scalar_prefetch=2,
        grid=(rows // tm, nf),
        in_specs=[pl.BlockSpec((tm, d), lambda n, f, te, tv: (n, 0)),
                  pl.BlockSpec((1, d), lambda n, f, te, tv: (0, 0)),
                  pl.BlockSpec((1, d, tf), lambda n, f, te, tv: (te[n], 0, f_eff(n, f, tv))),
                  pl.BlockSpec((1, d, tf), lambda n, f, te, tv: (te[n], 0, f_eff(n, f, tv))),
                  pl.BlockSpec((1, tf, d), lambda n, f, te, tv: (te[n], f_eff(n, f, tv), 0))],
        out_specs=pl.BlockSpec((tm, d), lambda n, f, te, tv: (n, 0)),
        scratch_shapes=[pltpu.VMEM((tm, d), MXU_DTYPE), pltpu.VMEM((tm, d), F32)],
    )
    return pl.pallas_call(
        _experts_kernel,
        out_shape=jax.ShapeDtypeStruct((rows, d), F32),
        grid_spec=grid_spec,
        compiler_params=_cparams(("arbitrary", "arbitrary")),
        name="moe_experts",
    )(tile_expert, tile_valid, xs, g, wg, wu, wd)


def _combine_kernel(dest_ref, y_ref, wts_ref, x_ref, g_ref, o_ref, buf_ref, sem, *, tm):
    def issue(t, carry):
        _row_copy(y_ref, dest_ref[0, 0, 2 * t], buf_ref.at[0], t, sem).start()
        _row_copy(y_ref, dest_ref[0, 0, 2 * t + 1], buf_ref.at[1], t, sem).start()
        return carry

    lax.fori_loop(0, tm, issue, 0)

    def drain(t, carry):
        _row_copy(y_ref, 0, buf_ref.at[0], 0, sem).wait()
        _row_copy(y_ref, 0, buf_ref.at[1], 0, sem).wait()
        return carry

    lax.fori_loop(0, tm, drain, 0)
    w = wts_ref[...]
    y = w[:, 0:1] * buf_ref[0] + w[:, 1:2] * buf_ref[1]
    o_ref[...] = x_ref[...] + _rms(y, g_ref[...])


def _combine(y, dest, wts, x, g, tm=256):
    m, d = x.shape
    row = lambda i: (i, 0)
    return pl.pallas_call(
        functools.partial(_combine_kernel, tm=tm),
        out_shape=jax.ShapeDtypeStruct((m, d), F32),
        grid=(m // tm,),
        in_specs=[pl.BlockSpec((1, 1, 2 * tm), lambda i: (i, 0, 0), memory_space=pltpu.SMEM),
                  pl.BlockSpec(memory_space=pl.ANY),
                  pl.BlockSpec((tm, LANES), row), pl.BlockSpec((tm, d), row),
                  pl.BlockSpec((1, d), lambda i: (0, 0))],
        out_specs=pl.BlockSpec((tm, d), row),
        scratch_shapes=[pltpu.VMEM((2, tm, d), F32), pltpu.SemaphoreType.DMA(())],
        compiler_params=_cparams(("arbitrary",)),
        name="moe_combine",
    )(dest.reshape(m // tm, 1, 2 * tm), y, wts, x, g)


def _rotary_tables(seq):
    pos = jnp.arange(seq, dtype=F32)
    inv = ROPE_THETA ** (-jnp.arange(0, ROT_DIM, 2, dtype=F32) / ROT_DIM)
    ang = pos[:, None] * inv[None, :]
    cos, sin = jnp.cos(ang), jnp.sin(ang)
    half = ROT_DIM // 2
    ones = jnp.ones((seq, HEAD_DIM - ROT_DIM), F32)
    zeros_h = jnp.zeros((seq, half), F32)
    zeros_t = jnp.zeros((seq, HEAD_DIM - ROT_DIM), F32)
    c = jnp.concatenate([cos, cos, ones], axis=1)
    s1 = jnp.concatenate([zeros_h, sin, zeros_t], axis=1)
    s2 = jnp.concatenate([-sin, zeros_h, zeros_t], axis=1)
    return c, s1, s2


def _pad_cols(w, n):
    return jnp.pad(w, ((0, 0), (0, n - w.shape[1])))


def _nsa_layer(x, hn, gains, w_in, pos_k, pos_v, wk1, wk2, wv1, wv2, w_out, batch, seq):
    qd, kvd = N_HEADS * HEAD_DIM, KV_GROUPS * HEAD_DIM
    cuts = [0, qd] + [qd + k * kvd for k in range(1, 7)] + [w_in.shape[1]]
    q_w, kc_w, vc_w, ksl_w, vsl_w, kw_w, vw_w, g_w = [w_in[:, a:b] for a, b in zip(cuts[:-1], cuts[1:])]
    tabs = _rotary_tables(seq)
    tn = 512
    w_main = jnp.concatenate([q_w, ksl_w, vsl_w, kw_w, vw_w], axis=1).astype(MXU_DTYPE)
    nq_tiles = qd // tn
    qkv = _proj_rot(hn, w_main, tabs, seq, rot_tiles=tuple(range(nq_tiles)) + (nq_tiles, nq_tiles + 2),
                    scale_tiles=tuple(range(nq_tiles)), scale=1.0 / math.sqrt(HEAD_DIM), out_dtype=MXU_DTYPE, tn=tn)
    w_c = jnp.concatenate([kc_w, vc_w], axis=1).astype(MXU_DTYPE)
    kcv = _proj_rot(hn, w_c, tabs, seq, rot_tiles=(0,), scale_tiles=(), scale=1.0, out_dtype=F32, tn=tn)
    gates = _proj_act(hn, _pad_cols(g_w, LANES).astype(MXU_DTYPE), jnp.zeros((1, LANES), F32),
                      act="sigmoid", out_dtype=F32)
    ck, cv = _compress(kcv, pos_k, pos_v, wk1.astype(MXU_DTYPE), wv1.astype(MXU_DTYPE),
                       wk2.astype(MXU_DTYPE), wv2.astype(MXU_DTYPE), batch, seq)
    attn = _nsa_attention(qkv, ck, cv, gates, batch, seq)
    return _outproj(attn, w_out.astype(MXU_DTYPE), x, gains[1:2], gains[2:3])


def _fox_layer(x, hn, gains, w_in, f_bias, w_out, batch, seq, tq=512):
    qkv_d = 3 * N_HEADS * HEAD_DIM
    tn = 512
    nq_tiles = N_HEADS * HEAD_DIM // tn
    qkv = _proj_rot(hn, w_in[:, :qkv_d].astype(MXU_DTYPE), (), seq, rot_tiles=(),
                    scale_tiles=tuple(range(nq_tiles)), scale=1.0 / math.sqrt(HEAD_DIM), out_dtype=MXU_DTYPE, tn=tn)
    bias = jnp.pad(f_bias.astype(F32), (0, LANES - N_HEADS)).reshape(1, LANES)
    log_f = _proj_act(hn, _pad_cols(w_in[:, qkv_d:], LANES).astype(MXU_DTYPE), bias,
                      act="log_sigmoid", out_dtype=F32)
    c_tok = _cumsum_rows(log_f, batch, seq)
    c_head = c_tok.reshape(batch, seq, LANES)[:, :, :N_HEADS].transpose(0, 2, 1).reshape(
        batch, N_HEADS, seq // tq, tq)
    attn = _fox_attention(qkv, c_tok, c_head, batch, seq, tq=tq)
    return _outproj(attn, w_out.astype(MXU_DTYPE), x, gains[1:2])


def _moe(x, gains, w_router, w_gate, w_up, w_down, tm=512):
    m, d = x.shape
    g_in, g_out = gains[2:3], gains[3:4]
    wts, meta, counts = _router(x, g_in, _pad_cols(w_router.astype(F32), LANES))
    i1, i2, r1, r2 = meta[:, 0], meta[:, 1], meta[:, 2], meta[:, 3]
    cnt = counts[0, :N_EXPERTS].astype(jnp.int32)
    padded = ((cnt + tm - 1) // tm) * tm
    ends = jnp.cumsum(padded)
    starts = ends - padded
    dest = jnp.stack([starts[i1] + r1, starts[i2] + r2], axis=1).reshape(-1).astype(jnp.int32)
    n_tiles = (2 * m) // tm + N_EXPERTS
    tile_start = jnp.arange(n_tiles, dtype=jnp.int32) * tm
    tile_valid = (tile_start < ends[-1]).astype(jnp.int32)
    last_live = jnp.maximum(ends[-1] // tm - 1, 0)
    tile_expert = jnp.sum(tile_start[:, None] >= ends[None, :], axis=1).astype(jnp.int32)
    tile_expert = jnp.where(tile_valid > 0, tile_expert, tile_expert[last_live]).astype(jnp.int32)
    tile_expert = jnp.minimum(tile_expert, N_EXPERTS - 1)
    xs = _dispatch(x, dest, n_tiles * tm)
    y = _experts(xs, g_in, w_gate.astype(MXU_DTYPE), w_up.astype(MXU_DTYPE), w_down.astype(MXU_DTYPE),
                 tile_expert, tile_valid, tm)
    return _combine(y, dest, wts, x, g_out)


def kernel(x, ln_gains, nsa_w_in, nsa_cmp_pos_k, nsa_cmp_pos_v, nsa_cmp_wk1, nsa_cmp_wk2, nsa_cmp_wv1, nsa_cmp_wv2,
           nsa_w_out, fox_w_in, fox_f_bias, fox_w_out, ffn_w_gate, ffn_w_up, ffn_w_down, moe_router, moe_w_gate,
           moe_w_up, moe_w_down):
    batch, seq, d = x.shape
    depth = ln_gains.shape[0]
    xf = x.reshape(batch * seq, d).astype(F32)
    hn = _rmsnorm(xf, ln_gains[0, 0:1])
    for i in range(depth):
        j = i // 2
        gains = ln_gains[i]
        last = i == depth - 1
        if i % 2 == 0:
            xf, hn2 = _nsa_layer(xf, hn, gains, nsa_w_in[j], nsa_cmp_pos_k[j], nsa_cmp_pos_v[j], nsa_cmp_wk1[j],
                                 nsa_cmp_wk2[j], nsa_cmp_wv1[j], nsa_cmp_wv2[j], nsa_w_out[j], batch, seq)
            g_next = ln_gains[i + 1, 0:1] if not last else gains[0:1]
            xf, hn = _ffn(hn2, ffn_w_gate[j].astype(MXU_DTYPE), ffn_w_up[j].astype(MXU_DTYPE),
                          ffn_w_down[j].astype(MXU_DTYPE), xf, gains[3:4], g_next)
        else:
            xf = _fox_layer(xf, hn, gains, fox_w_in[j], fox_f_bias[j], fox_w_out[j], batch, seq)
            xf = _moe(xf, gains, moe_router[j], moe_w_gate[j], moe_w_up[j], moe_w_down[j])
            if not last:
                hn = _rmsnorm(xf, ln_gains[i + 1, 0:1])
    return xf.reshape(batch, seq, d).astype(x.dtype)
```

```python
import functools
import math

import jax
import jax.numpy as jnp
from jax import lax
from jax.experimental import pallas as pl
from jax.experimental.pallas import tpu as pltpu

N_HEADS = 16
HEAD_DIM = 128
ROT_DIM = HEAD_DIM // 4
ROPE_THETA = 500000.0
KV_GROUPS = 4
HEADS_PER_GROUP = N_HEADS // KV_GROUPS
CMP_LEN = 32
CMP_STRIDE = 16
SLC_LEN = 64
N_SEL = 16
WINDOW = 512
FORCE_BONUS = 1000.0
N_EXPERTS = 8
RMS_EPS = 1e-6
NEG_INF = -1e30
LOG2E = math.log2(math.e)
Q_SCALE = LOG2E / math.sqrt(HEAD_DIM)
LANES = 128
MXU_DTYPE = jnp.bfloat16
VMEM_LIMIT = 56 * 1024 * 1024
F32 = jnp.float32

_NT = (((1,), (1,)), ((), ()))


def _cparams(sem):
    return pltpu.CompilerParams(dimension_semantics=sem, vmem_limit_bytes=VMEM_LIMIT)


def _rms(x, g):
    ms = jnp.mean(x * x, axis=-1, keepdims=True)
    return x * lax.rsqrt(ms + RMS_EPS) * g


def _split3(x):
    a = x.astype(MXU_DTYPE)
    r = x - a.astype(F32)
    b = r.astype(MXU_DTYPE)
    c = (r - b.astype(F32)).astype(MXU_DTYPE)
    return a, b, c


def _rmsnorm_kernel(x_ref, g_ref, o_ref):
    o_ref[...] = _rms(x_ref[...], g_ref[...]).astype(o_ref.dtype)


def _rmsnorm(x, g, tm=512):
    m, d = x.shape
    return pl.pallas_call(
        _rmsnorm_kernel,
        out_shape=jax.ShapeDtypeStruct((m, d), MXU_DTYPE),
        grid=(m // tm,),
        in_specs=[pl.BlockSpec((tm, d), lambda i: (i, 0)), pl.BlockSpec((1, d), lambda i: (0, 0))],
        out_specs=pl.BlockSpec((tm, d), lambda i: (i, 0)),
        compiler_params=_cparams(("parallel",)),
        name="rmsnorm",
    )(x, g)


def _any_tile(j, tiles):
    return functools.reduce(jnp.logical_or, [j == t for t in tiles])


def _proj_rot_kernel(a_ref, w_ref, *rest, rot_tiles, scale_tiles, scale):
    o_ref = rest[-1]
    j = pl.program_id(1)
    out = jnp.dot(a_ref[...], w_ref[...], preferred_element_type=F32)
    tn = out.shape[1]
    if rot_tiles:
        c_ref, s1_ref, s2_ref = rest[:3]
        rotf = jnp.where(_any_tile(j, rot_tiles), 1.0, 0.0).astype(F32)
        reps = tn // HEAD_DIM
        c = jnp.concatenate([1.0 + rotf * (c_ref[...] - 1.0)] * reps, axis=1)
        s1 = jnp.concatenate([rotf * s1_ref[...]] * reps, axis=1)
        s2 = jnp.concatenate([rotf * s2_ref[...]] * reps, axis=1)
        half = ROT_DIM // 2
        out = out * c + pltpu.roll(out, half, 1) * s1 + pltpu.roll(out, tn - half, 1) * s2
    if scale_tiles:
        out = out * jnp.where(_any_tile(j, scale_tiles), scale, 1.0).astype(F32)
    o_ref[...] = out.astype(o_ref.dtype)


def _proj_rot(a, w, tabs, seq, *, rot_tiles, scale_tiles, scale, out_dtype, tm=512, tn=512):
    m, k = a.shape
    n = w.shape[1]
    nrow = seq // tm
    tab_spec = pl.BlockSpec((tm, HEAD_DIM), lambda i, j: (i % nrow, 0))
    tabs = tuple(tabs) if rot_tiles else ()
    return pl.pallas_call(
        functools.partial(_proj_rot_kernel, rot_tiles=rot_tiles, scale_tiles=scale_tiles, scale=scale),
        out_shape=jax.ShapeDtypeStruct((m, n), out_dtype),
        grid=(m // tm, n // tn),
        in_specs=[pl.BlockSpec((tm, k), lambda i, j: (i, 0)),
                  pl.BlockSpec((k, tn), lambda i, j: (0, j))] + [tab_spec] * len(tabs),
        out_specs=pl.BlockSpec((tm, tn), lambda i, j: (i, j)),
        compiler_params=_cparams(("parallel", "arbitrary")),
        name="proj_rot" if rot_tiles else "proj_plain",
    )(a, w, *tabs)


def _proj_act_kernel(a_ref, w_ref, b_ref, o_ref, *, act):
    acc = jnp.dot(a_ref[...], w_ref[...], preferred_element_type=F32) + b_ref[...]
    if act == "sigmoid":
        acc = jax.nn.sigmoid(acc)
    elif act == "log_sigmoid":
        acc = jax.nn.log_sigmoid(acc)
    o_ref[...] = acc.astype(o_ref.dtype)


def _proj_act(a, w, b, *, act, out_dtype, tm=512, tn=512):
    m, k = a.shape
    n = w.shape[1]
    tn = min(tn, n)
    return pl.pallas_call(
        functools.partial(_proj_act_kernel, act=act),
        out_shape=jax.ShapeDtypeStruct((m, n), out_dtype),
        grid=(m // tm, n // tn),
        in_specs=[pl.BlockSpec((tm, k), lambda i, j: (i, 0)),
                  pl.BlockSpec((k, tn), lambda i, j: (0, j)),
                  pl.BlockSpec((1, tn), lambda i, j: (0, j))],
        out_specs=pl.BlockSpec((tm, tn), lambda i, j: (i, j)),
        compiler_params=_cparams(("parallel", "arbitrary")),
        name="proj_act",
    )(a, w, b)


def _compress_kernel(x_ref, posk_ref, posv_ref, wka_ref, wkb_ref, wva_ref, wvb_ref, w2k_ref, w2v_ref,
                     ck_ref, cv_ref, acc_ref):
    l = pl.program_id(1)
    nl = pl.num_programs(1)

    @pl.when(l == 0)
    def _():
        acc_ref[...] = jnp.zeros_like(acc_ref)

    x = x_ref[0]
    pos = (posk_ref, posv_ref)
    wa = (wka_ref, wva_ref)
    wb = (wkb_ref, wvb_ref)
    for kv in range(2):
        pa = pos[kv][pl.ds(l, 1), :]
        pb = pos[kv][pl.ds(l + CMP_STRIDE, 1), :]
        for g in range(KV_GROUPS):
            off = (kv * KV_GROUPS + g) * HEAD_DIM
            xg = x[:, off:off + HEAD_DIM]
            acc_ref[kv, g, 0] += jnp.dot((xg + pa).astype(MXU_DTYPE), wa[kv][...], preferred_element_type=F32)
            acc_ref[kv, g, 1] += jnp.dot((xg + pb).astype(MXU_DTYPE), wb[kv][...], preferred_element_type=F32)

    @pl.when(l == nl - 1)
    def _():
        rows = acc_ref.shape[3]
        w2 = (w2k_ref, w2v_ref)
        outs = (ck_ref, cv_ref)
        for kv in range(2):
            for g in range(KV_GROUPS):
                nxt = pltpu.roll(acc_ref[kv, g, 1], rows - 1, 0)
                h = jax.nn.gelu(acc_ref[kv, g, 0] + nxt)
                outs[kv][0, g] = jnp.dot(h.astype(MXU_DTYPE), w2[kv][...],
                                         preferred_element_type=F32).astype(outs[kv].dtype)


def _compress(kcv, pos_k, pos_v, w1k, w1v, w2k, w2v, batch, seq):
    chunks = seq // CMP_STRIDE
    width = 2 * KV_GROUPS * HEAD_DIM
    x = kcv.reshape(batch, chunks, CMP_STRIDE * width)
    hid = w1k.shape[1]
    wa_spec = pl.BlockSpec((HEAD_DIM, hid), lambda b, l: (l, 0))
    wb_spec = pl.BlockSpec((HEAD_DIM, hid), lambda b, l: (l + CMP_STRIDE, 0))
    full = lambda shp: pl.BlockSpec(shp, lambda b, l: (0,) * len(shp))
    out_sds = jax.ShapeDtypeStruct((batch, KV_GROUPS, chunks, HEAD_DIM), MXU_DTYPE)
    out_spec = pl.BlockSpec((1, KV_GROUPS, chunks, HEAD_DIM), lambda b, l: (b, 0, 0, 0))
    return pl.pallas_call(
        _compress_kernel,
        out_shape=(out_sds, out_sds),
        grid=(batch, CMP_STRIDE),
        in_specs=[pl.BlockSpec((1, chunks, width), lambda b, l: (b, 0, l)),
                  full(pos_k.shape), full(pos_v.shape),
                  wa_spec, wb_spec, wa_spec, wb_spec,
                  full(w2k.shape), full(w2v.shape)],
        out_specs=(out_spec, out_spec),
        scratch_shapes=[pltpu.VMEM((2, KV_GROUPS, 2, chunks, hid), F32)],
        compiler_params=_cparams(("parallel", "arbitrary")),
        name="nsa_compress",
    )(x, pos_k, pos_v, w1k, w1k, w1v, w1v, w2k, w2v)


def _softmax_update(s, v, m_scr, l_scr, acc_scr, row_shift=None):
    reps = s.shape[1] // LANES
    m_prev = m_scr[...]
    m_cur = jnp.max(s, axis=1, keepdims=True)
    if row_shift is not None:
        m_cur = m_cur + row_shift
    m_new = jnp.maximum(m_prev, m_cur)
    alpha = jnp.exp2(m_prev - m_new)
    off = m_new if row_shift is None else m_new - row_shift
    p = jnp.exp2(s - jnp.concatenate([off] * reps, axis=1))
    l_scr[...] = alpha * l_scr[...] + jnp.sum(p, axis=1, keepdims=True)
    acc_scr[...] = alpha * acc_scr[...] + jnp.dot(p.astype(MXU_DTYPE), v, preferred_element_type=F32)
    m_scr[...] = m_new


def _nsa_attn_kernel(q_ref, ksl_ref, vsl_ref, kw_ref, vw_ref, ck_ref, cv_ref, gate_ref, ovl_ref, exp_ref,
                     o_ref, m_scr, l_scr, acc_scr, *, tq, tk, ns, n_sel):
    g = pl.program_id(1)
    qi = pl.program_id(2)
    q0 = qi * tq
    rph = HEADS_PER_GROUP
    rows = rph * tq
    qb = q_ref[...]
    qrows = jnp.concatenate([qb[:, r * HEAD_DIM:(r + 1) * HEAD_DIM] for r in range(rph)], axis=0)
    qpos = q0 + lax.broadcasted_iota(jnp.int32, (tq, 1), 0)

    ck = ck_ref[0, 0]
    ncp = ck.shape[0]
    s = lax.dot_general(qrows, ck, _NT, preferred_element_type=F32).reshape(rph, tq, ncp)
    cidx = lax.broadcasted_iota(jnp.int32, (tq, ncp), 1)
    valid = cidx * CMP_STRIDE + (CMP_LEN - 1) <= qpos
    s = s + jnp.where(valid, 0.0, NEG_INF)[None]
    e = jnp.exp2(s - jnp.max(s, axis=-1, keepdims=True))
    den = jnp.maximum(jnp.sum(e, axis=-1, keepdims=True), 1e-30)
    any_valid = (qpos >= CMP_LEN - 1)[None]
    p = e * jnp.where(any_valid, 1.0 / den, 0.0)
    o_cmp = jnp.dot(p.reshape(rows, ncp).astype(MXU_DTYPE), cv_ref[0, 0], preferred_element_type=F32)

    psum = p[0]
    for r in range(1, rph):
        psum = psum + p[r]
    hi = psum.astype(MXU_DTYPE)
    lo = (psum - hi.astype(F32)).astype(MXU_DTYPE)
    ovl = ovl_ref[...]
    imp = jnp.dot(hi, ovl, preferred_element_type=F32) + jnp.dot(lo, ovl, preferred_element_type=F32)
    nidx = lax.broadcasted_iota(jnp.int32, (tq, LANES), 1)
    jq = lax.shift_right_logical(qpos, int(math.log2(SLC_LEN)))
    forced = (nidx == 0) | (nidx == jq) | (nidx == jq - 1)
    score = jnp.where(nidx <= jq, imp + jnp.where(forced, FORCE_BONUS, 0.0), -1.0)
    score = jnp.where(nidx < ns, score, -2.0)
    s_t = score.T
    sub = 8
    nsp = ((ns + sub - 1) // sub) * sub
    cands = [s_t[j:j + sub] for j in range(0, nsp, sub)]
    sub_io = lax.broadcasted_iota(jnp.int32, (sub, tq), 0)
    cnts = [jnp.zeros((sub, tq), jnp.int32) for _ in cands]
    for mm in range(ns):
        row = s_t[mm:mm + 1, :]
        for jb, cand in enumerate(cands):
            lo_n = jb * sub
            if lo_n > mm:
                beats = row >= cand
            elif lo_n + sub - 1 <= mm:
                beats = row > cand
            else:
                beats = (row > cand) | ((row == cand) & (sub_io > mm - lo_n))
            cnts[jb] = cnts[jb] + jnp.where(beats, 1, 0)
    cnt = jnp.concatenate(cnts, axis=0)
    sel_t = jnp.where(cnt < n_sel, 1.0, 0.0).astype(F32)
    if nsp < LANES:
        sel_t = jnp.concatenate([sel_t, jnp.zeros((LANES - nsp, tq), F32)], axis=0)
    sel = sel_t.T.astype(MXU_DTYPE)

    m_scr[...] = jnp.full_like(m_scr, NEG_INF)
    l_scr[...] = jnp.zeros_like(l_scr)
    acc_scr[...] = jnp.zeros_like(acc_scr)

    def slc_body(kt, carry):
        k0 = pl.multiple_of(kt * tk, tk)
        kb = ksl_ref[pl.ds(k0, tk), :]
        vb = vsl_ref[pl.ds(k0, tk), :]
        sc = lax.dot_general(qrows, kb, _NT, preferred_element_type=F32).reshape(rph, tq, tk)
        selx = jnp.dot(sel, exp_ref[kt], preferred_element_type=F32)
        kpos = k0 + lax.broadcasted_iota(jnp.int32, (tq, tk), 1)
        bias = jnp.where((selx > 0.5) & (kpos <= qpos), 0.0, NEG_INF)
        sc = (sc + bias[None]).reshape(rows, tk)
        _softmax_update(sc, vb, m_scr, l_scr, acc_scr)
        return carry

    lax.fori_loop(0, (q0 + tq + tk - 1) // tk, slc_body, 0)
    o_slc = acc_scr[...] / jnp.maximum(l_scr[...], 1e-30)

    wlen = tq + WINDOW
    st = pl.multiple_of(jnp.maximum(q0 - WINDOW, 0), tq)
    kwb = kw_ref[pl.ds(st, wlen), :]
    vwb = vw_ref[pl.ds(st, wlen), :]
    sw = lax.dot_general(qrows, kwb, _NT, preferred_element_type=F32).reshape(rph, tq, wlen)
    kpos = st + lax.broadcasted_iota(jnp.int32, (tq, wlen), 1)
    okw = (kpos <= qpos) & (kpos > qpos - WINDOW)
    sw = sw + jnp.where(okw, 0.0, NEG_INF)[None]
    pw = jnp.exp2(sw - jnp.max(sw, axis=-1, keepdims=True))
    lw = jnp.maximum(jnp.sum(pw, axis=-1, keepdims=True), 1e-30).reshape(rows, 1)
    o_win = jnp.dot(pw.reshape(rows, wlen).astype(MXU_DTYPE), vwb, preferred_element_type=F32) * (1.0 / lw)

    gt = gate_ref[...]
    outs = []
    for r in range(rph):
        sl = slice(r * tq, (r + 1) * tq)
        acc = None
        for br, ob in enumerate((o_cmp, o_slc, o_win)):
            lane_id = br * N_HEADS + g * rph + r
            gcol = jnp.sum(jnp.where(nidx == lane_id, gt, 0.0), axis=1, keepdims=True)
            term = gcol * ob[sl]
            acc = term if acc is None else acc + term
        outs.append(acc)
    o_ref[...] = jnp.concatenate(outs, axis=1).astype(o_ref.dtype)


def _nsa_attention(qkv, ck, cv, gates, batch, seq, tq=256, tk=512):
    t = batch * seq
    nq = seq // tq
    ns = seq // SLC_LEN
    n_sel = min(N_SEL, ns)
    ncp = ck.shape[2]
    gw = HEADS_PER_GROUP * HEAD_DIM
    c = jnp.arange(ncp)
    n = jnp.arange(LANES)
    cs, ce = c * CMP_STRIDE, c * CMP_STRIDE + CMP_LEN - 1
    nc = (seq - CMP_LEN) // CMP_STRIDE + 1
    ovl = ((cs[:, None] < n[None, :] * SLC_LEN + SLC_LEN) & (ce[:, None] >= n[None, :] * SLC_LEN)
           & (c[:, None] < nc) & (n[None, :] < ns)).astype(MXU_DTYPE)
    kp = jnp.arange(seq).reshape(seq // tk, 1, tk)
    expand = (kp // SLC_LEN == n[None, :, None]).astype(MXU_DTYPE)
    kv_col0 = N_HEADS

    def kv_spec(which):
        return pl.BlockSpec((seq, HEAD_DIM), lambda b, g, i: (b, kv_col0 + which * KV_GROUPS + g))

    cmp_spec = pl.BlockSpec((1, 1, ncp, HEAD_DIM), lambda b, g, i: (b, g, 0, 0))
    rows = HEADS_PER_GROUP * tq
    return pl.pallas_call(
        functools.partial(_nsa_attn_kernel, tq=tq, tk=tk, ns=ns, n_sel=n_sel),
        out_shape=jax.ShapeDtypeStruct((t, N_HEADS * HEAD_DIM), MXU_DTYPE),
        grid=(batch, KV_GROUPS, nq),
        in_specs=[pl.BlockSpec((tq, gw), lambda b, g, i: (b * nq + i, g)),
                  kv_spec(0), kv_spec(1), kv_spec(2), kv_spec(3),
                  cmp_spec, cmp_spec,
                  pl.BlockSpec((tq, LANES), lambda b, g, i: (b * nq + i, 0)),
                  pl.BlockSpec(ovl.shape, lambda b, g, i: (0, 0)),
                  pl.BlockSpec(expand.shape, lambda b, g, i: (0, 0, 0))],
        out_specs=pl.BlockSpec((tq, gw), lambda b, g, i: (b * nq + i, g)),
        scratch_shapes=[pltpu.VMEM((rows, LANES), F32), pltpu.VMEM((rows, LANES), F32),
                        pltpu.VMEM((rows, HEAD_DIM), F32)],
        compiler_params=_cparams(("parallel", "parallel", "arbitrary")),
        name="nsa_attention",
    )(qkv, qkv, qkv, qkv, qkv, ck, cv, gates, ovl, expand)


def _cumsum_kernel(x_ref, o_ref, carry_ref):
    @pl.when(pl.program_id(1) == 0)
    def _():
        carry_ref[...] = jnp.zeros_like(carry_ref)

    x = x_ref[...]
    n = x.shape[0]
    tri = (lax.broadcasted_iota(jnp.int32, (n, n), 1) <= lax.broadcasted_iota(jnp.int32, (n, n), 0))
    tri = jnp.where(tri, 1.0, 0.0).astype(MXU_DTYPE)
    a, b, c = _split3(x)
    inc = (jnp.dot(tri, a, preferred_element_type=F32) + jnp.dot(tri, b, preferred_element_type=F32)
           + jnp.dot(tri, c, preferred_element_type=F32))
    out = inc + carry_ref[...]
    o_ref[...] = out
    carry_ref[...] = out[n - 1:n, :]


def _cumsum_rows(x, batch, seq, tb=512):
    nb = seq // tb
    return pl.pallas_call(
        _cumsum_kernel,
        out_shape=jax.ShapeDtypeStruct(x.shape, F32),
        grid=(batch, nb),
        in_specs=[pl.BlockSpec((tb, x.shape[1]), lambda b, i: (b * nb + i, 0))],
        out_specs=pl.BlockSpec((tb, x.shape[1]), lambda b, i: (b * nb + i, 0)),
        scratch_shapes=[pltpu.VMEM((1, x.shape[1]), F32)],
        compiler_params=_cparams(("parallel", "arbitrary")),
        name="fox_cumsum",
    )(x)


def _fox_attn_kernel(q_ref, k_ref, v_ref, cq_ref, ck_ref, o_ref, m_scr, l_scr, acc_scr, *, tq, hpb):
    hb = pl.program_id(1)
    qi = pl.program_id(2)
    lane = lax.broadcasted_iota(jnp.int32, (tq, LANES), 1)
    cq_all = cq_ref[...]
    hd = lambda j: slice(j * HEAD_DIM, (j + 1) * HEAD_DIM)
    qs = [q_ref[:, hd(j)] for j in range(hpb)]
    cqs = [jnp.sum(jnp.where(lane == hb * hpb + j, cq_all, 0.0), axis=1, keepdims=True) * LOG2E for j in range(hpb)]
    m_scr[...] = jnp.full_like(m_scr, NEG_INF)
    l_scr[...] = jnp.zeros_like(l_scr)
    acc_scr[...] = jnp.zeros_like(acc_scr)

    def step(kt, causal):
        k0 = pl.multiple_of(kt * tq, tq)
        for j in range(hpb):
            s = lax.dot_general(qs[j], k_ref[pl.ds(k0, tq), hd(j)], _NT, preferred_element_type=F32)
            s = s - ck_ref[0, j, pl.ds(kt, 1), :] * LOG2E
            if causal is not None:
                s = jnp.where(causal, s, NEG_INF)
            _softmax_update(s, v_ref[pl.ds(k0, tq), hd(j)], m_scr.at[j], l_scr.at[j], acc_scr.at[j], row_shift=cqs[j])

    def body(kt, carry):
        step(kt, None)
        return carry

    lax.fori_loop(0, qi, body, 0)
    step(qi, lax.broadcasted_iota(jnp.int32, (tq, tq), 1) <= lax.broadcasted_iota(jnp.int32, (tq, tq), 0))
    for j in range(hpb):
        o_ref[:, hd(j)] = (acc_scr[j] / jnp.maximum(l_scr[j], 1e-30)).astype(o_ref.dtype)


def _fox_attention(qkv, c_tok, c_head, batch, seq, tq=512, hpb=4):
    t = batch * seq
    nq = seq // tq
    nhb = N_HEADS // hpb
    w = hpb * HEAD_DIM
    return pl.pallas_call(
        functools.partial(_fox_attn_kernel, tq=tq, hpb=hpb),
        out_shape=jax.ShapeDtypeStruct((t, N_HEADS * HEAD_DIM), MXU_DTYPE),
        grid=(batch, nhb, nq),
        in_specs=[pl.BlockSpec((tq, w), lambda b, h, i: (b * nq + i, h)),
                  pl.BlockSpec((seq, w), lambda b, h, i: (b, nhb + h)),
                  pl.BlockSpec((seq, w), lambda b, h, i: (b, 2 * nhb + h)),
                  pl.BlockSpec((tq, LANES), lambda b, h, i: (b * nq + i, 0)),
                  pl.BlockSpec((1, hpb, nq, tq), lambda b, h, i: (b, h, 0, 0))],
        out_specs=pl.BlockSpec((tq, w), lambda b, h, i: (b * nq + i, h)),
        scratch_shapes=[pltpu.VMEM((hpb, tq, LANES), F32), pltpu.VMEM((hpb, tq, LANES), F32),
                        pltpu.VMEM((hpb, tq, HEAD_DIM), F32)],
        compiler_params=_cparams(("parallel", "parallel", "arbitrary")),
        name="fox_attention",
    )(qkv, qkv, qkv, c_tok, c_head)


def _outproj_kernel(a_ref, w_ref, x_ref, g1_ref, *rest):
    y = jnp.dot(a_ref[...], w_ref[...], preferred_element_type=F32)
    xn = x_ref[...] + _rms(y, g1_ref[...])
    if len(rest) == 1:
        rest[0][...] = xn
    else:
        g2_ref, xo_ref, hn_ref = rest
        xo_ref[...] = xn
        hn_ref[...] = _rms(xn, g2_ref[...]).astype(hn_ref.dtype)


def _outproj(a, w, x, g1, g2=None, tm=256):
    m, k = a.shape
    d = w.shape[1]
    row = lambda i: (i, 0)
    fix = lambda i: (0, 0)
    in_specs = [pl.BlockSpec((tm, k), row), pl.BlockSpec((k, d), fix), pl.BlockSpec((tm, d), row),
                pl.BlockSpec((1, d), fix)]
    x_sds, x_spec = jax.ShapeDtypeStruct((m, d), F32), pl.BlockSpec((tm, d), row)
    if g2 is None:
        args, out_shape, out_specs = (a, w, x, g1), x_sds, x_spec
    else:
        args = (a, w, x, g1, g2)
        in_specs = in_specs + [pl.BlockSpec((1, d), fix)]
        out_shape = (x_sds, jax.ShapeDtypeStruct((m, d), MXU_DTYPE))
        out_specs = (x_spec, pl.BlockSpec((tm, d), row))
    return pl.pallas_call(
        _outproj_kernel,
        out_shape=out_shape,
        grid=(m // tm,),
        in_specs=in_specs,
        out_specs=out_specs,
        compiler_params=_cparams(("parallel",)),
        name="outproj_norm",
    )(*args)


def _ffn_kernel(h_ref, wg_ref, wu_ref, wd_ref, x_ref, g1_ref, g2_ref, xo_ref, hn_ref, acc_ref):
    f = pl.program_id(1)

    @pl.when(f == 0)
    def _():
        acc_ref[...] = jnp.zeros_like(acc_ref)

    h = h_ref[...]
    a = jnp.dot(h, wg_ref[...], preferred_element_type=F32)
    u = jnp.dot(h, wu_ref[...], preferred_element_type=F32)
    acc_ref[...] += jnp.dot((jax.nn.silu(a) * u).astype(MXU_DTYPE), wd_ref[...], preferred_element_type=F32)

    @pl.when(f == pl.num_programs(1) - 1)
    def _():
        xn = x_ref[...] + _rms(acc_ref[...], g1_ref[...])
        xo_ref[...] = xn
        hn_ref[...] = _rms(xn, g2_ref[...]).astype(hn_ref.dtype)


def _ffn(h, wg, wu, wd, x, g1, g2, tm=512, tf=512):
    m, d = h.shape
    ff = wg.shape[1]
    row = lambda i, f: (i, 0)
    fix = lambda i, f: (0, 0)
    return pl.pallas_call(
        _ffn_kernel,
        out_shape=(jax.ShapeDtypeStruct((m, d), F32), jax.ShapeDtypeStruct((m, d), MXU_DTYPE)),
        grid=(m // tm, ff // tf),
        in_specs=[pl.BlockSpec((tm, d), row),
                  pl.BlockSpec((d, tf), lambda i, f: (0, f)),
                  pl.BlockSpec((d, tf), lambda i, f: (0, f)),
                  pl.BlockSpec((tf, d), lambda i, f: (f, 0)),
                  pl.BlockSpec((tm, d), row), pl.BlockSpec((1, d), fix), pl.BlockSpec((1, d), fix)],
        out_specs=(pl.BlockSpec((tm, d), row), pl.BlockSpec((tm, d), row)),
        scratch_shapes=[pltpu.VMEM((tm, d), F32)],
        compiler_params=_cparams(("parallel", "arbitrary")),
        name="ffn_swiglu",
    )(h, wg, wu, wd, x, g1, g2)


def _router_kernel(x_ref, g_ref, wr_ref, wts_ref, meta_ref, cnt_ref, carry_ref):
    @pl.when(pl.program_id(0) == 0)
    def _():
        carry_ref[...] = jnp.zeros_like(carry_ref)

    hn = _rms(x_ref[...], g_ref[...])
    tm = hn.shape[0]
    logits = jnp.dot(hn, wr_ref[...], preferred_element_type=F32, precision=lax.Precision.HIGHEST)
    lane = lax.broadcasted_iota(jnp.int32, (tm, LANES), 1)
    lanef = lane.astype(F32)
    lg = jnp.where(lane < N_EXPERTS, logits, -jnp.inf)
    v1 = jnp.max(lg, axis=1, keepdims=True)
    i1 = jnp.min(jnp.where(lg == v1, lanef, float(LANES)), axis=1, keepdims=True)
    lg2 = jnp.where(lanef == i1, -jnp.inf, lg)
    v2 = jnp.max(lg2, axis=1, keepdims=True)
    i2 = jnp.min(jnp.where(lg2 == v2, lanef, float(LANES)), axis=1, keepdims=True)
    e2 = jnp.exp(v2 - v1)
    den = 1.0 + e2
    wts_ref[...] = jnp.where(lane == 0, 1.0 / den, jnp.where(lane == 1, e2 / den, 0.0))

    onehot = jnp.where((lanef == i1) | (lanef == i2), 1.0, 0.0)
    tri = (lax.broadcasted_iota(jnp.int32, (tm, tm), 1) <= lax.broadcasted_iota(jnp.int32, (tm, tm), 0))
    incl = jnp.dot(jnp.where(tri, 1.0, 0.0).astype(MXU_DTYPE), onehot.astype(MXU_DTYPE),
                   preferred_element_type=F32)
    rank = carry_ref[...] + incl - onehot
    r1 = jnp.sum(jnp.where(lanef == i1, rank, 0.0), axis=1, keepdims=True)
    r2 = jnp.sum(jnp.where(lanef == i2, rank, 0.0), axis=1, keepdims=True)
    meta = jnp.where(lane == 0, i1, jnp.where(lane == 1, i2, jnp.where(lane == 2, r1,
                     jnp.where(lane == 3, r2, 0.0))))
    meta_ref[...] = meta.astype(jnp.int32)
    total = carry_ref[...] + incl[tm - 1:tm, :]
    carry_ref[...] = total
    cnt_ref[...] = total


def _router(x, g, wr, tm=512):
    m, d = x.shape
    row = lambda i: (i, 0)
    fix = lambda i: (0, 0)
    return pl.pallas_call(
        _router_kernel,
        out_shape=(jax.ShapeDtypeStruct((m, LANES), F32), jax.ShapeDtypeStruct((m, LANES), jnp.int32),
                   jax.ShapeDtypeStruct((1, LANES), F32)),
        grid=(m // tm,),
        in_specs=[pl.BlockSpec((tm, d), row), pl.BlockSpec((1, d), fix), pl.BlockSpec((d, LANES), fix)],
        out_specs=(pl.BlockSpec((tm, LANES), row), pl.BlockSpec((tm, LANES), row), pl.BlockSpec((1, LANES), fix)),
        scratch_shapes=[pltpu.VMEM((1, LANES), F32)],
        compiler_params=_cparams(("arbitrary",)),
        name="moe_router",
    )(x, g, wr)


def _row_copy(src_ref, s, dst_ref, d, sem):
    return pltpu.make_async_copy(src_ref.at[pl.ds(s, 1)], dst_ref.at[pl.ds(d, 1)], sem)


def _dispatch_kernel(dest_ref, x_ref, zeros_ref, xs_ref, sem, *, tc):
    del zeros_ref

    def issue(t, carry):
        _row_copy(x_ref, t, xs_ref, dest_ref[0, 0, 2 * t], sem).start()
        _row_copy(x_ref, t, xs_ref, dest_ref[0, 0, 2 * t + 1], sem).start()
        return carry

    lax.fori_loop(0, tc, issue, 0, unroll=8)

    def drain(t, carry):
        _row_copy(x_ref, 0, xs_ref, 0, sem).wait()
        _row_copy(x_ref, 0, xs_ref, 0, sem).wait()
        return carry

    lax.fori_loop(0, tc, drain, 0, unroll=8)


def _dispatch(x, dest, rows_total, tc=512):
    m, d = x.shape
    zeros = jnp.zeros((rows_total, d), x.dtype)
    return pl.pallas_call(
        functools.partial(_dispatch_kernel, tc=tc),
        out_shape=jax.ShapeDtypeStruct((rows_total, d), x.dtype),
        grid=(m // tc,),
        in_specs=[pl.BlockSpec((1, 1, 2 * tc), lambda i: (i, 0, 0), memory_space=pltpu.SMEM),
                  pl.BlockSpec((tc, d), lambda i: (i, 0)), pl.BlockSpec(memory_space=pl.ANY)],
        out_specs=pl.BlockSpec(memory_space=pl.ANY),
        scratch_shapes=[pltpu.SemaphoreType.DMA(())],
        input_output_aliases={2: 0},
        compiler_params=_cparams(("arbitrary",)),
        name="moe_dispatch",
    )(dest.reshape(m // tc, 1, 2 * tc), x, zeros)


def _experts_kernel(te_ref, tv_ref, xs_ref, g_ref, wg_ref, wu_ref, wd_ref, o_ref, hn_ref, acc_ref):
    n = pl.program_id(0)
    f = pl.program_id(1)
    live = tv_ref[n] > 0

    @pl.when(live & (f == 0))
    def _():
        hn_ref[...] = _rms(xs_ref[...], g_ref[...]).astype(hn_ref.dtype)
        acc_ref[...] = jnp.zeros_like(acc_ref)

    @pl.when(live)
    def _():
        h = hn_ref[...]
        a = jnp.dot(h, wg_ref[0], preferred_element_type=F32)
        u = jnp.dot(h, wu_ref[0], preferred_element_type=F32)
        acc_ref[...] += jnp.dot((jax.nn.silu(a) * u).astype(MXU_DTYPE), wd_ref[0], preferred_element_type=F32)

    last = f == pl.num_programs(1) - 1

    @pl.when(live & last)
    def _():
        o_ref[...] = acc_ref[...]

    @pl.when(jnp.logical_not(live) & last)
    def _():
        o_ref[...] = jnp.zeros_like(o_ref)


def _experts(xs, g, wg, wu, wd, tile_expert, tile_valid, tm, tf=512):
    rows, d = xs.shape
    ff = wg.shape[2]
    nf = ff // tf

    def f_eff(n, f, tv):
        return jnp.where(tv[n] > 0, f, nf - 1)

    grid_spec = pltpu.PrefetchScalarGridSpec(
        num_scalar_prefetch=2,
        grid=(rows // tm, nf),
        in_specs=[pl.BlockSpec((tm, d), lambda n, f, te, tv: (n, 0)),
                  pl.BlockSpec((1, d), lambda n, f, te, tv: (0, 0)),
                  pl.BlockSpec((1, d, tf), lambda n, f, te, tv: (te[n], 0, f_eff(n, f, tv))),
                  pl.BlockSpec((1, d, tf), lambda n, f, te, tv: (te[n], 0, f_eff(n, f, tv))),
                  pl.BlockSpec((1, tf, d), lambda n, f, te, tv: (te[n], f_eff(n, f, tv), 0))],
        out_specs=pl.BlockSpec((tm, d), lambda n, f, te, tv: (n, 0)),
        scratch_shapes=[pltpu.VMEM((tm, d), MXU_DTYPE), pltpu.VMEM((tm, d), F32)],
    )
    return pl.pallas_call(
        _experts_kernel,
        out_shape=jax.ShapeDtypeStruct((rows, d), F32),
        grid_spec=grid_spec,
        compiler_params=_cparams(("arbitrary", "arbitrary")),
        name="moe_experts",
    )(tile_expert, tile_valid, xs, g, wg, wu, wd)


def _combine_kernel(dest_ref, y_ref, wts_ref, x_ref, g_ref, o_ref, buf_ref, sem, *, tm):
    def issue(t, carry):
        _row_copy(y_ref, dest_ref[0, 0, 2 * t], buf_ref.at[0], t, sem).start()
        _row_copy(y_ref, dest_ref[0, 0, 2 * t + 1], buf_ref.at[1], t, sem).start()
        return carry

    lax.fori_loop(0, tm, issue, 0, unroll=8)

    def drain(t, carry):
        _row_copy(y_ref, 0, buf_ref.at[0], 0, sem).wait()
        _row_copy(y_ref, 0, buf_ref.at[1], 0, sem).wait()
        return carry

    lax.fori_loop(0, tm, drain, 0, unroll=8)
    w = wts_ref[...]
    y = w[:, 0:1] * buf_ref[0] + w[:, 1:2] * buf_ref[1]
    o_ref[...] = x_ref[...] + _rms(y, g_ref[...])


def _combine(y, dest, wts, x, g, tm=256):
    m, d = x.shape
    row = lambda i: (i, 0)
    return pl.pallas_call(
        functools.partial(_combine_kernel, tm=tm),
        out_shape=jax.ShapeDtypeStruct((m, d), F32),
        grid=(m // tm,),
        in_specs=[pl.BlockSpec((1, 1, 2 * tm), lambda i: (i, 0, 0), memory_space=pltpu.SMEM),
                  pl.BlockSpec(memory_space=pl.ANY),
                  pl.BlockSpec((tm, LANES), row), pl.BlockSpec((tm, d), row),
                  pl.BlockSpec((1, d), lambda i: (0, 0))],
        out_specs=pl.BlockSpec((tm, d), row),
        scratch_shapes=[pltpu.VMEM((2, tm, d), F32), pltpu.SemaphoreType.DMA(())],
        compiler_params=_cparams(("arbitrary",)),
        name="moe_combine",
    )(dest.reshape(m // tm, 1, 2 * tm), y, wts, x, g)


def _rotary_tables(seq):
    pos = jnp.arange(seq, dtype=F32)
    inv = ROPE_THETA ** (-jnp.arange(0, ROT_DIM, 2, dtype=F32) / ROT_DIM)
    ang = pos[:, None] * inv[None, :]
    cos, sin = jnp.cos(ang), jnp.sin(ang)
    half = ROT_DIM // 2
    ones = jnp.ones((seq, HEAD_DIM - ROT_DIM), F32)
    zeros_h = jnp.zeros((seq, half), F32)
    zeros_t = jnp.zeros((seq, HEAD_DIM - ROT_DIM), F32)
    c = jnp.concatenate([cos, cos, ones], axis=1)
    s1 = jnp.concatenate([zeros_h, sin, zeros_t], axis=1)
    s2 = jnp.concatenate([-sin, zeros_h, zeros_t], axis=1)
    return c, s1, s2


def _pad_cols(w, n):
    return jnp.pad(w, ((0, 0), (0, n - w.shape[1])))


def _nsa_layer(x, hn, gains, w_in, pos_k, pos_v, wk1, wk2, wv1, wv2, w_out, batch, seq):
    qd, kvd = N_HEADS * HEAD_DIM, KV_GROUPS * HEAD_DIM
    cuts = [0, qd] + [qd + k * kvd for k in range(1, 7)] + [w_in.shape[1]]
    q_w, kc_w, vc_w, ksl_w, vsl_w, kw_w, vw_w, g_w = [w_in[:, a:b] for a, b in zip(cuts[:-1], cuts[1:])]
    tabs = _rotary_tables(seq)
    tn = 512
    w_main = jnp.concatenate([q_w, ksl_w, vsl_w, kw_w, vw_w], axis=1).astype(MXU_DTYPE)
    nq_tiles = qd // tn
    qkv = _proj_rot(hn, w_main, tabs, seq, rot_tiles=tuple(range(nq_tiles)) + (nq_tiles, nq_tiles + 2),
                    scale_tiles=tuple(range(nq_tiles)), scale=Q_SCALE, out_dtype=MXU_DTYPE, tn=tn)
    w_c = jnp.concatenate([kc_w, vc_w], axis=1).astype(MXU_DTYPE)
    kcv = _proj_rot(hn, w_c, tabs, seq, rot_tiles=(0,), scale_tiles=(), scale=1.0, out_dtype=F32, tn=tn)
    gates = _proj_act(hn, _pad_cols(g_w, LANES).astype(MXU_DTYPE), jnp.zeros((1, LANES), F32),
                      act="sigmoid", out_dtype=F32)
    ck, cv = _compress(kcv, pos_k, pos_v, wk1.astype(MXU_DTYPE), wv1.astype(MXU_DTYPE),
                       wk2.astype(MXU_DTYPE), wv2.astype(MXU_DTYPE), batch, seq)
    attn = _nsa_attention(qkv, ck, cv, gates, batch, seq)
    return _outproj(attn, w_out.astype(MXU_DTYPE), x, gains[1:2], gains[2:3])


def _fox_layer(x, hn, gains, w_in, f_bias, w_out, batch, seq, tq=512):
    qkv_d = 3 * N_HEADS * HEAD_DIM
    tn = 512
    nq_tiles = N_HEADS * HEAD_DIM // tn
    qkv = _proj_rot(hn, w_in[:, :qkv_d].astype(MXU_DTYPE), (), seq, rot_tiles=(),
                    scale_tiles=tuple(range(nq_tiles)), scale=Q_SCALE, out_dtype=MXU_DTYPE, tn=tn)
    bias = jnp.pad(f_bias.astype(F32), (0, LANES - N_HEADS)).reshape(1, LANES)
    log_f = _proj_act(hn, _pad_cols(w_in[:, qkv_d:], LANES).astype(MXU_DTYPE), bias,
                      act="log_sigmoid", out_dtype=F32)
    c_tok = _cumsum_rows(log_f, batch, seq)
    c_head = c_tok.reshape(batch, seq, LANES)[:, :, :N_HEADS].transpose(0, 2, 1).reshape(
        batch, N_HEADS, seq // tq, tq)
    attn = _fox_attention(qkv, c_tok, c_head, batch, seq, tq=tq)
    return _outproj(attn, w_out.astype(MXU_DTYPE), x, gains[1:2])


def _moe(x, gains, w_router, w_gate, w_up, w_down, tm=512):
    m, d = x.shape
    g_in, g_out = gains[2:3], gains[3:4]
    wts, meta, counts = _router(x, g_in, _pad_cols(w_router.astype(F32), LANES))
    i1, i2, r1, r2 = meta[:, 0], meta[:, 1], meta[:, 2], meta[:, 3]
    cnt = counts[0, :N_EXPERTS].astype(jnp.int32)
    padded = ((cnt + tm - 1) // tm) * tm
    ends = jnp.cumsum(padded)
    starts = ends - padded
    dest = jnp.stack([starts[i1] + r1, starts[i2] + r2], axis=1).reshape(-1).astype(jnp.int32)
    n_tiles = (2 * m) // tm + N_EXPERTS
    tile_start = jnp.arange(n_tiles, dtype=jnp.int32) * tm
    tile_valid = (tile_start < ends[-1]).astype(jnp.int32)
    last_live = jnp.maximum(ends[-1] // tm - 1, 0)
    tile_expert = jnp.sum(tile_start[:, None] >= ends[None, :], axis=1).astype(jnp.int32)
    tile_expert = jnp.where(tile_valid > 0, tile_expert, tile_expert[last_live]).astype(jnp.int32)
    tile_expert = jnp.minimum(tile_expert, N_EXPERTS - 1)
    xs = _dispatch(x, dest, n_tiles * tm)
    y = _experts(xs, g_in, w_gate.astype(MXU_DTYPE), w_up.astype(MXU_DTYPE), w_down.astype(MXU_DTYPE),
                 tile_expert, tile_valid, tm)
    return _combine(y, dest, wts, x, g_out)


def kernel(x, ln_gains, nsa_w_in, nsa_cmp_pos_k, nsa_cmp_pos_v, nsa_cmp_wk1, nsa_cmp_wk2, nsa_cmp_wv1, nsa_cmp_wv2,
           nsa_w_out, fox_w_in, fox_f_bias, fox_w_out, ffn_w_gate, ffn_w_up, ffn_w_down, moe_router, moe_w_gate,
           moe_w_up, moe_w_down):
    batch, seq, d = x.shape
    depth = ln_gains.shape[0]
    xf = x.reshape(batch * seq, d).astype(F32)
    hn = _rmsnorm(xf, ln_gains[0, 0:1])
    for i in range(depth):
        j = i // 2
        gains = ln_gains[i]
        last = i == depth - 1
        if i % 2 == 0:
            xf, hn2 = _nsa_layer(xf, hn, gains, nsa_w_in[j], nsa_cmp_pos_k[j], nsa_cmp_pos_v[j], nsa_cmp_wk1[j],
                                 nsa_cmp_wk2[j], nsa_cmp_wv1[j], nsa_cmp_wv2[j], nsa_w_out[j], batch, seq)
            g_next = ln_gains[i + 1, 0:1] if not last else gains[0:1]
            xf, hn = _ffn(hn2, ffn_w_gate[j].astype(MXU_DTYPE), ffn_w_up[j].astype(MXU_DTYPE),
                          ffn_w_down[j].astype(MXU_DTYPE), xf, gains[3:4], g_next)
        else:
            xf = _fox_layer(xf, hn, gains, fox_w_in[j], fox_f_bias[j], fox_w_out[j], batch, seq)
            xf = _moe(xf, gains, moe_router[j], moe_w_gate[j], moe_w_up[j], moe_w_down[j])
            if not last:
                hn = _rmsnorm(xf, ln_gains[i + 1, 0:1])
    return xf.reshape(batch, seq, d).astype(x.dtype)
```

```python
import functools
import math

import jax
import jax.numpy as jnp
from jax import lax
from jax.experimental import pallas as pl
from jax.experimental.pallas import tpu as pltpu

N_HEADS = 16
HEAD_DIM = 128
ROT_DIM = HEAD_DIM // 4
ROPE_THETA = 500000.0
KV_GROUPS = 4
HEADS_PER_GROUP = N_HEADS // KV_GROUPS
CMP_LEN = 32
CMP_STRIDE = 16
SLC_LEN = 64
N_SEL = 16
WINDOW = 512
FORCE_BONUS = 1000.0
N_EXPERTS = 8
RMS_EPS = 1e-6
NEG_INF = -1e30
LOG2E = math.log2(math.e)
Q_SCALE = LOG2E / math.sqrt(HEAD_DIM)
LANES = 128
MXU_DTYPE = jnp.bfloat16
VMEM_LIMIT = 56 * 1024 * 1024
F32 = jnp.float32

_NT = (((1,), (1,)), ((), ()))


def _cparams(sem):
    return pltpu.CompilerParams(dimension_semantics=sem, vmem_limit_bytes=VMEM_LIMIT)


def _rms(x, g):
    ms = jnp.mean(x * x, axis=-1, keepdims=True)
    return x * lax.rsqrt(ms + RMS_EPS) * g


def _split3(x):
    a = x.astype(MXU_DTYPE)
    r = x - a.astype(F32)
    b = r.astype(MXU_DTYPE)
    c = (r - b.astype(F32)).astype(MXU_DTYPE)
    return a, b, c


def _rmsnorm_kernel(x_ref, g_ref, o_ref):
    o_ref[...] = _rms(x_ref[...], g_ref[...]).astype(o_ref.dtype)


def _rmsnorm(x, g, tm=512):
    m, d = x.shape
    return pl.pallas_call(
        _rmsnorm_kernel,
        out_shape=jax.ShapeDtypeStruct((m, d), MXU_DTYPE),
        grid=(m // tm,),
        in_specs=[pl.BlockSpec((tm, d), lambda i: (i, 0)), pl.BlockSpec((1, d), lambda i: (0, 0))],
        out_specs=pl.BlockSpec((tm, d), lambda i: (i, 0)),
        compiler_params=_cparams(("parallel",)),
        name="rmsnorm",
    )(x, g)


def _any_tile(j, tiles):
    return functools.reduce(jnp.logical_or, [j == t for t in tiles])


PROJ_SUB = 512


def _proj_rot_kernel(a_ref, w_ref, *rest, rot_tiles, scale_tiles, scale):
    o_ref = rest[-1]
    j = pl.program_id(1)
    a = a_ref[...]
    nsub = o_ref.shape[1] // PROJ_SUB
    for cb in range(nsub):
        cols = slice(cb * PROJ_SUB, (cb + 1) * PROJ_SUB)
        blk = j * nsub + cb
        out = jnp.dot(a, w_ref[:, cols], preferred_element_type=F32)
        if rot_tiles:
            c_ref, s1_ref, s2_ref = rest[:3]
            rotf = jnp.where(_any_tile(blk, rot_tiles), 1.0, 0.0).astype(F32)
            reps = PROJ_SUB // HEAD_DIM
            c = jnp.concatenate([1.0 + rotf * (c_ref[...] - 1.0)] * reps, axis=1)
            s1 = jnp.concatenate([rotf * s1_ref[...]] * reps, axis=1)
            s2 = jnp.concatenate([rotf * s2_ref[...]] * reps, axis=1)
            half = ROT_DIM // 2
            out = out * c + pltpu.roll(out, half, 1) * s1 + pltpu.roll(out, PROJ_SUB - half, 1) * s2
        if scale_tiles:
            out = out * jnp.where(_any_tile(blk, scale_tiles), scale, 1.0).astype(F32)
        o_ref[:, cols] = out.astype(o_ref.dtype)


def _proj_rot(a, w, tabs, seq, *, rot_tiles, scale_tiles, scale, out_dtype, tm=1024, tn=1024):
    m, k = a.shape
    n = w.shape[1]
    nrow = seq // tm
    tab_spec = pl.BlockSpec((tm, HEAD_DIM), lambda i, j: (i % nrow, 0))
    tabs = tuple(tabs) if rot_tiles else ()
    return pl.pallas_call(
        functools.partial(_proj_rot_kernel, rot_tiles=rot_tiles, scale_tiles=scale_tiles, scale=scale),
        out_shape=jax.ShapeDtypeStruct((m, n), out_dtype),
        grid=(m // tm, n // tn),
        in_specs=[pl.BlockSpec((tm, k), lambda i, j: (i, 0)),
                  pl.BlockSpec((k, tn), lambda i, j: (0, j))] + [tab_spec] * len(tabs),
        out_specs=pl.BlockSpec((tm, tn), lambda i, j: (i, j)),
        compiler_params=_cparams(("parallel", "arbitrary")),
        name="proj_rot" if rot_tiles else "proj_plain",
    )(a, w, *tabs)


def _proj_act_kernel(a_ref, w_ref, b_ref, o_ref, *, act):
    acc = jnp.dot(a_ref[...], w_ref[...], preferred_element_type=F32) + b_ref[...]
    if act == "sigmoid":
        acc = jax.nn.sigmoid(acc)
    elif act == "log_sigmoid":
        acc = jax.nn.log_sigmoid(acc)
    o_ref[...] = acc.astype(o_ref.dtype)


def _proj_act(a, w, b, *, act, out_dtype, tm=1024, tn=512):
    m, k = a.shape
    n = w.shape[1]
    tn = min(tn, n)
    return pl.pallas_call(
        functools.partial(_proj_act_kernel, act=act),
        out_shape=jax.ShapeDtypeStruct((m, n), out_dtype),
        grid=(m // tm, n // tn),
        in_specs=[pl.BlockSpec((tm, k), lambda i, j: (i, 0)),
                  pl.BlockSpec((k, tn), lambda i, j: (0, j)),
                  pl.BlockSpec((1, tn), lambda i, j: (0, j))],
        out_specs=pl.BlockSpec((tm, tn), lambda i, j: (i, j)),
        compiler_params=_cparams(("parallel", "arbitrary")),
        name="proj_act",
    )(a, w, b)


def _compress_kernel(x_ref, posk_ref, posv_ref, wka_ref, wkb_ref, wva_ref, wvb_ref, w2k_ref, w2v_ref,
                     ck_ref, cv_ref, acc_ref):
    l = pl.program_id(1)
    nl = pl.num_programs(1)

    @pl.when(l == 0)
    def _():
        acc_ref[...] = jnp.zeros_like(acc_ref)

    x = x_ref[0]
    pos = (posk_ref, posv_ref)
    wa = (wka_ref, wva_ref)
    wb = (wkb_ref, wvb_ref)
    for kv in range(2):
        pa = pos[kv][pl.ds(l, 1), :]
        pb = pos[kv][pl.ds(l + CMP_STRIDE, 1), :]
        for g in range(KV_GROUPS):
            off = (kv * KV_GROUPS + g) * HEAD_DIM
            xg = x[:, off:off + HEAD_DIM]
            acc_ref[kv, g, 0] += jnp.dot((xg + pa).astype(MXU_DTYPE), wa[kv][...], preferred_element_type=F32)
            acc_ref[kv, g, 1] += jnp.dot((xg + pb).astype(MXU_DTYPE), wb[kv][...], preferred_element_type=F32)

    @pl.when(l == nl - 1)
    def _():
        rows = acc_ref.shape[3]
        w2 = (w2k_ref, w2v_ref)
        outs = (ck_ref, cv_ref)
        for kv in range(2):
            for g in range(KV_GROUPS):
                nxt = pltpu.roll(acc_ref[kv, g, 1], rows - 1, 0)
                h = jax.nn.gelu(acc_ref[kv, g, 0] + nxt)
                outs[kv][0, g] = jnp.dot(h.astype(MXU_DTYPE), w2[kv][...],
                                         preferred_element_type=F32).astype(outs[kv].dtype)


def _compress(kcv, pos_k, pos_v, w1k, w1v, w2k, w2v, batch, seq):
    chunks = seq // CMP_STRIDE
    width = 2 * KV_GROUPS * HEAD_DIM
    x = kcv.reshape(batch, chunks, CMP_STRIDE * width)
    hid = w1k.shape[1]
    wa_spec = pl.BlockSpec((HEAD_DIM, hid), lambda b, l: (l, 0))
    wb_spec = pl.BlockSpec((HEAD_DIM, hid), lambda b, l: (l + CMP_STRIDE, 0))
    full = lambda shp: pl.BlockSpec(shp, lambda b, l: (0,) * len(shp))
    out_sds = jax.ShapeDtypeStruct((batch, KV_GROUPS, chunks, HEAD_DIM), MXU_DTYPE)
    out_spec = pl.BlockSpec((1, KV_GROUPS, chunks, HEAD_DIM), lambda b, l: (b, 0, 0, 0))
    return pl.pallas_call(
        _compress_kernel,
        out_shape=(out_sds, out_sds),
        grid=(batch, CMP_STRIDE),
        in_specs=[pl.BlockSpec((1, chunks, width), lambda b, l: (b, 0, l)),
                  full(pos_k.shape), full(pos_v.shape),
                  wa_spec, wb_spec, wa_spec, wb_spec,
                  full(w2k.shape), full(w2v.shape)],
        out_specs=(out_spec, out_spec),
        scratch_shapes=[pltpu.VMEM((2, KV_GROUPS, 2, chunks, hid), F32)],
        compiler_params=_cparams(("parallel", "arbitrary")),
        name="nsa_compress",
    )(x, pos_k, pos_v, w1k, w1k, w1v, w1v, w2k, w2v)


def _scores(q, kb, *, bias=None, col_sub=None):
    s = lax.dot_general(q, kb, _NT, preferred_element_type=F32)
    if col_sub is not None:
        s = s - col_sub
    if bias is not None:
        rows, tk = s.shape
        br = bias.shape[0]
        s = (s.reshape(rows // br, br, tk) + bias[None]).reshape(rows, tk)
    return s


def _softmax_pv(s, vb, m_scr, l_scr, acc_scr, row_shift=None):
    reps = s.shape[1] // LANES
    m_prev = m_scr[...]
    m_cur = jnp.max(s, axis=1, keepdims=True)
    if row_shift is not None:
        m_cur = m_cur + row_shift
    m_new = jnp.maximum(m_prev, m_cur)
    alpha = jnp.exp2(m_prev - m_new)
    off = m_new if row_shift is None else m_new - row_shift
    p = jnp.exp2(s - jnp.concatenate([off] * reps, axis=1))
    l_scr[...] = alpha * l_scr[...] + jnp.sum(p, axis=1, keepdims=True)
    acc_scr[...] = alpha * acc_scr[...] + jnp.dot(p.astype(MXU_DTYPE), vb, preferred_element_type=F32)
    m_scr[...] = m_new


def _nsa_attn_kernel(q_ref, ksl_ref, vsl_ref, kw_ref, vw_ref, ck_ref, cv_ref, gate_ref, ovl_ref, exp_ref,
                     o_ref, m_scr, l_scr, acc_scr, win_scr, *, tq, tk, ns, n_sel):
    g = pl.program_id(1)
    qi = pl.program_id(2)
    q0 = qi * tq
    rph = HEADS_PER_GROUP
    rows = rph * tq
    qb = q_ref[...]
    qrows = jnp.concatenate([qb[:, r * HEAD_DIM:(r + 1) * HEAD_DIM] for r in range(rph)], axis=0)
    qpos = q0 + lax.broadcasted_iota(jnp.int32, (tq, 1), 0)

    wlen = tq + WINDOW
    st = pl.multiple_of(jnp.maximum(q0 - WINDOW, 0), tq)
    kwb = kw_ref[pl.ds(st, wlen), :]
    vwb = vw_ref[pl.ds(st, wlen), :]
    sw = lax.dot_general(qrows, kwb, _NT, preferred_element_type=F32).reshape(rph, tq, wlen)
    kpos = st + lax.broadcasted_iota(jnp.int32, (tq, wlen), 1)
    okw = (kpos <= qpos) & (kpos > qpos - WINDOW)
    sw = sw + jnp.where(okw, 0.0, NEG_INF)[None]
    pw = jnp.exp2(sw - jnp.max(sw, axis=-1, keepdims=True))
    lw = jnp.maximum(jnp.sum(pw, axis=-1, keepdims=True), 1e-30).reshape(rows, 1)
    win_scr[...] = jnp.dot(pw.reshape(rows, wlen).astype(MXU_DTYPE), vwb, preferred_element_type=F32) * (1.0 / lw)

    ck = ck_ref[0, 0]
    ncp = ck.shape[0]
    s = lax.dot_general(qrows, ck, _NT, preferred_element_type=F32).reshape(rph, tq, ncp)
    cidx = lax.broadcasted_iota(jnp.int32, (tq, ncp), 1)
    valid = cidx * CMP_STRIDE + (CMP_LEN - 1) <= qpos
    s = s + jnp.where(valid, 0.0, NEG_INF)[None]
    e = jnp.exp2(s - jnp.max(s, axis=-1, keepdims=True))
    den = jnp.maximum(jnp.sum(e, axis=-1, keepdims=True), 1e-30)
    any_valid = (qpos >= CMP_LEN - 1)[None]
    p = e * jnp.where(any_valid, 1.0 / den, 0.0)

    psum = p[0]
    for r in range(1, rph):
        psum = psum + p[r]
    hi = psum.astype(MXU_DTYPE)
    lo = (psum - hi.astype(F32)).astype(MXU_DTYPE)
    ovl = ovl_ref[...]
    imp = jnp.dot(hi, ovl, preferred_element_type=F32) + jnp.dot(lo, ovl, preferred_element_type=F32)
    o_cmp = jnp.dot(p.reshape(rows, ncp).astype(MXU_DTYPE), cv_ref[0, 0], preferred_element_type=F32)
    nidx = lax.broadcasted_iota(jnp.int32, (tq, LANES), 1)
    jq = lax.shift_right_logical(qpos, int(math.log2(SLC_LEN)))
    forced = (nidx == 0) | (nidx == jq) | (nidx == jq - 1)
    score = jnp.where(nidx <= jq, imp + jnp.where(forced, FORCE_BONUS, 0.0), -1.0)
    score = jnp.where(nidx < ns, score, -2.0)
    s_t = score.T
    sub = 8
    nsp = ((ns + sub - 1) // sub) * sub
    cands = [s_t[j:j + sub] for j in range(0, nsp, sub)]
    sub_io = lax.broadcasted_iota(jnp.int32, (sub, tq), 0)
    cnts = [jnp.zeros((sub, tq), jnp.int32) for _ in cands]
    for mm in range(ns):
        row = s_t[mm:mm + 1, :]
        for jb, cand in enumerate(cands):
            lo_n = jb * sub
            if lo_n > mm:
                beats = row >= cand
            elif lo_n + sub - 1 <= mm:
                beats = row > cand
            else:
                beats = (row > cand) | ((row == cand) & (sub_io > mm - lo_n))
            cnts[jb] = cnts[jb] + jnp.where(beats, 1, 0)
    cnt = jnp.concatenate(cnts, axis=0)
    sel_t = jnp.where(cnt < n_sel, 1.0, 0.0).astype(F32)
    if nsp < LANES:
        sel_t = jnp.concatenate([sel_t, jnp.zeros((LANES - nsp, tq), F32)], axis=0)
    sel = sel_t.T.astype(MXU_DTYPE)

    m_scr[...] = jnp.full_like(m_scr, NEG_INF)
    l_scr[...] = jnp.zeros_like(l_scr)
    acc_scr[...] = jnp.zeros_like(acc_scr)

    def slc_body(kt, carry):
        k0 = pl.multiple_of(kt * tk, tk)
        kb = ksl_ref[pl.ds(k0, tk), :]
        vb = vsl_ref[pl.ds(k0, tk), :]
        selx = jnp.dot(sel, exp_ref[kt], preferred_element_type=F32)
        kpos = k0 + lax.broadcasted_iota(jnp.int32, (tq, tk), 1)
        bias = jnp.where((selx > 0.5) & (kpos <= qpos), 0.0, NEG_INF)
        _softmax_pv(_scores(qrows, kb, bias=bias), vb, m_scr, l_scr, acc_scr)
        return carry

    lax.fori_loop(0, (q0 + tq + tk - 1) // tk, slc_body, 0)
    o_slc = acc_scr[...] / jnp.maximum(l_scr[...], 1e-30)
    o_win = win_scr[...]

    gt = gate_ref[...]
    outs = []
    for r in range(rph):
        sl = slice(r * tq, (r + 1) * tq)
        acc = None
        for br, ob in enumerate((o_cmp, o_slc, o_win)):
            lane_id = br * N_HEADS + g * rph + r
            gcol = jnp.sum(jnp.where(nidx == lane_id, gt, 0.0), axis=1, keepdims=True)
            term = gcol * ob[sl]
            acc = term if acc is None else acc + term
        outs.append(acc)
    o_ref[...] = jnp.concatenate(outs, axis=1).astype(o_ref.dtype)


def _nsa_attention(qkv, ck, cv, gates, batch, seq, tq=256, tk=512):
    t = batch * seq
    nq = seq // tq
    ns = seq // SLC_LEN
    n_sel = min(N_SEL, ns)
    ncp = ck.shape[2]
    gw = HEADS_PER_GROUP * HEAD_DIM
    c = jnp.arange(ncp)
    n = jnp.arange(LANES)
    cs, ce = c * CMP_STRIDE, c * CMP_STRIDE + CMP_LEN - 1
    nc = (seq - CMP_LEN) // CMP_STRIDE + 1
    ovl = ((cs[:, None] < n[None, :] * SLC_LEN + SLC_LEN) & (ce[:, None] >= n[None, :] * SLC_LEN)
           & (c[:, None] < nc) & (n[None, :] < ns)).astype(MXU_DTYPE)
    kp = jnp.arange(seq).reshape(seq // tk, 1, tk)
    expand = (kp // SLC_LEN == n[None, :, None]).astype(MXU_DTYPE)
    kv_col0 = N_HEADS

    def kv_spec(which):
        return pl.BlockSpec((seq, HEAD_DIM), lambda b, g, i: (b, kv_col0 + which * KV_GROUPS + g))

    cmp_spec = pl.BlockSpec((1, 1, ncp, HEAD_DIM), lambda b, g, i: (b, g, 0, 0))
    rows = HEADS_PER_GROUP * tq
    return pl.pallas_call(
        functools.partial(_nsa_attn_kernel, tq=tq, tk=tk, ns=ns, n_sel=n_sel),
        out_shape=jax.ShapeDtypeStruct((t, N_HEADS * HEAD_DIM), MXU_DTYPE),
        grid=(batch, KV_GROUPS, nq),
        in_specs=[pl.BlockSpec((tq, gw), lambda b, g, i: (b * nq + i, g)),
                  kv_spec(0), kv_spec(1), kv_spec(2), kv_spec(3),
                  cmp_spec, cmp_spec,
                  pl.BlockSpec((tq, LANES), lambda b, g, i: (b * nq + i, 0)),
                  pl.BlockSpec(ovl.shape, lambda b, g, i: (0, 0)),
                  pl.BlockSpec(expand.shape, lambda b, g, i: (0, 0, 0))],
        out_specs=pl.BlockSpec((tq, gw), lambda b, g, i: (b * nq + i, g)),
        scratch_shapes=[pltpu.VMEM((rows, LANES), F32), pltpu.VMEM((rows, LANES), F32),
                        pltpu.VMEM((rows, HEAD_DIM), F32), pltpu.VMEM((rows, HEAD_DIM), F32)],
        compiler_params=_cparams(("parallel", "parallel", "arbitrary")),
        name="nsa_attention",
    )(qkv, qkv, qkv, qkv, qkv, ck, cv, gates, ovl, expand)


def _cumsum_kernel(x_ref, o_ref, carry_ref):
    @pl.when(pl.program_id(1) == 0)
    def _():
        carry_ref[...] = jnp.zeros_like(carry_ref)

    x = x_ref[...]
    n = x.shape[0]
    tri = (lax.broadcasted_iota(jnp.int32, (n, n), 1) <= lax.broadcasted_iota(jnp.int32, (n, n), 0))
    tri = jnp.where(tri, 1.0, 0.0).astype(MXU_DTYPE)
    a, b, c = _split3(x)
    inc = (jnp.dot(tri, a, preferred_element_type=F32) + jnp.dot(tri, b, preferred_element_type=F32)
           + jnp.dot(tri, c, preferred_element_type=F32))
    out = inc + carry_ref[...]
    o_ref[...] = out
    carry_ref[...] = out[n - 1:n, :]


def _cumsum_rows(x, batch, seq, tb=512):
    nb = seq // tb
    return pl.pallas_call(
        _cumsum_kernel,
        out_shape=jax.ShapeDtypeStruct(x.shape, F32),
        grid=(batch, nb),
        in_specs=[pl.BlockSpec((tb, x.shape[1]), lambda b, i: (b * nb + i, 0))],
        out_specs=pl.BlockSpec((tb, x.shape[1]), lambda b, i: (b * nb + i, 0)),
        scratch_shapes=[pltpu.VMEM((1, x.shape[1]), F32)],
        compiler_params=_cparams(("parallel", "arbitrary")),
        name="fox_cumsum",
    )(x)


def _fox_attn_kernel(q_ref, k_ref, v_ref, cq_ref, ck_ref, o_ref, m_scr, l_scr, acc_scr, *, tq, hpb):
    hb = pl.program_id(1)
    qi = pl.program_id(2)
    lane = lax.broadcasted_iota(jnp.int32, (tq, LANES), 1)
    cq_all = cq_ref[...]
    hd = lambda j: slice(j * HEAD_DIM, (j + 1) * HEAD_DIM)
    qs = [q_ref[:, hd(j)] for j in range(hpb)]
    cqs = [jnp.sum(jnp.where(lane == hb * hpb + j, cq_all, 0.0), axis=1, keepdims=True) * LOG2E for j in range(hpb)]
    m_scr[...] = jnp.full_like(m_scr, NEG_INF)
    l_scr[...] = jnp.zeros_like(l_scr)
    acc_scr[...] = jnp.zeros_like(acc_scr)

    def step(kt, causal):
        k0 = pl.multiple_of(kt * tq, tq)
        sc = lambda j: _scores(qs[j], k_ref[pl.ds(k0, tq), hd(j)], col_sub=ck_ref[0, j, pl.ds(kt, 1), :] * LOG2E)
        s_next = sc(0)
        for j in range(hpb):
            s = s_next
            if j + 1 < hpb:
                s_next = sc(j + 1)
            if causal is not None:
                s = jnp.where(causal, s, NEG_INF)
            _softmax_pv(s, v_ref[pl.ds(k0, tq), hd(j)], m_scr.at[j], l_scr.at[j], acc_scr.at[j], row_shift=cqs[j])

    def body(kt, carry):
        step(kt, None)
        return carry

    lax.fori_loop(0, qi, body, 0)
    step(qi, lax.broadcasted_iota(jnp.int32, (tq, tq), 1) <= lax.broadcasted_iota(jnp.int32, (tq, tq), 0))
    for j in range(hpb):
        o_ref[:, hd(j)] = (acc_scr[j] / jnp.maximum(l_scr[j], 1e-30)).astype(o_ref.dtype)


def _fox_attention(qkv, c_tok, c_head, batch, seq, tq=512, hpb=4):
    t = batch * seq
    nq = seq // tq
    nhb = N_HEADS // hpb
    w = hpb * HEAD_DIM
    return pl.pallas_call(
        functools.partial(_fox_attn_kernel, tq=tq, hpb=hpb),
        out_shape=jax.ShapeDtypeStruct((t, N_HEADS * HEAD_DIM), MXU_DTYPE),
        grid=(batch, nhb, nq),
        in_specs=[pl.BlockSpec((tq, w), lambda b, h, i: (b * nq + i, h)),
                  pl.BlockSpec((seq, w), lambda b, h, i: (b, nhb + h)),
                  pl.BlockSpec((seq, w), lambda b, h, i: (b, 2 * nhb + h)),
                  pl.BlockSpec((tq, LANES), lambda b, h, i: (b * nq + i, 0)),
                  pl.BlockSpec((1, hpb, nq, tq), lambda b, h, i: (b, h, 0, 0))],
        out_specs=pl.BlockSpec((tq, w), lambda b, h, i: (b * nq + i, h)),
        scratch_shapes=[pltpu.VMEM((hpb, tq, LANES), F32), pltpu.VMEM((hpb, tq, LANES), F32),
                        pltpu.VMEM((hpb, tq, HEAD_DIM), F32)],
        compiler_params=_cparams(("parallel", "parallel", "arbitrary")),
        name="fox_attention",
    )(qkv, qkv, qkv, c_tok, c_head)


def _outproj_kernel(a_ref, w_ref, x_ref, g1_ref, *rest):
    y = jnp.dot(a_ref[...], w_ref[...], preferred_element_type=F32)
    xn = x_ref[...] + _rms(y, g1_ref[...])
    if len(rest) == 1:
        rest[0][...] = xn
    else:
        g2_ref, xo_ref, hn_ref = rest
        xo_ref[...] = xn
        hn_ref[...] = _rms(xn, g2_ref[...]).astype(hn_ref.dtype)


def _outproj(a, w, x, g1, g2=None, tm=512):
    m, k = a.shape
    d = w.shape[1]
    row = lambda i: (i, 0)
    fix = lambda i: (0, 0)
    in_specs = [pl.BlockSpec((tm, k), row), pl.BlockSpec((k, d), fix), pl.BlockSpec((tm, d), row),
                pl.BlockSpec((1, d), fix)]
    x_sds, x_spec = jax.ShapeDtypeStruct((m, d), F32), pl.BlockSpec((tm, d), row)
    if g2 is None:
        args, out_shape, out_specs = (a, w, x, g1), x_sds, x_spec
    else:
        args = (a, w, x, g1, g2)
        in_specs = in_specs + [pl.BlockSpec((1, d), fix)]
        out_shape = (x_sds, jax.ShapeDtypeStruct((m, d), MXU_DTYPE))
        out_specs = (x_spec, pl.BlockSpec((tm, d), row))
    return pl.pallas_call(
        _outproj_kernel,
        out_shape=out_shape,
        grid=(m // tm,),
        in_specs=in_specs,
        out_specs=out_specs,
        compiler_params=_cparams(("parallel",)),
        name="outproj_norm",
    )(*args)


def _ffn_kernel(h_ref, wg_ref, wu_ref, wd_ref, x_ref, g1_ref, g2_ref, xo_ref, hn_ref, acc_ref):
    f = pl.program_id(1)

    @pl.when(f == 0)
    def _():
        acc_ref[...] = jnp.zeros_like(acc_ref)

    h = h_ref[...]
    a = jnp.dot(h, wg_ref[...], preferred_element_type=F32)
    u = jnp.dot(h, wu_ref[...], preferred_element_type=F32)
    acc_ref[...] += jnp.dot((jax.nn.silu(a) * u).astype(MXU_DTYPE), wd_ref[...], preferred_element_type=F32)

    @pl.when(f == pl.num_programs(1) - 1)
    def _():
        xn = x_ref[...] + _rms(acc_ref[...], g1_ref[...])
        xo_ref[...] = xn
        hn_ref[...] = _rms(xn, g2_ref[...]).astype(hn_ref.dtype)


def _ffn(h, wg, wu, wd, x, g1, g2, tm=512, tf=512):
    m, d = h.shape
    ff = wg.shape[1]
    row = lambda i, f: (i, 0)
    fix = lambda i, f: (0, 0)
    return pl.pallas_call(
        _ffn_kernel,
        out_shape=(jax.ShapeDtypeStruct((m, d), F32), jax.ShapeDtypeStruct((m, d), MXU_DTYPE)),
        grid=(m // tm, ff // tf),
        in_specs=[pl.BlockSpec((tm, d), row),
                  pl.BlockSpec((d, tf), lambda i, f: (0, f)),
                  pl.BlockSpec((d, tf), lambda i, f: (0, f)),
                  pl.BlockSpec((tf, d), lambda i, f: (f, 0)),
                  pl.BlockSpec((tm, d), row), pl.BlockSpec((1, d), fix), pl.BlockSpec((1, d), fix)],
        out_specs=(pl.BlockSpec((tm, d), row), pl.BlockSpec((tm, d), row)),
        scratch_shapes=[pltpu.VMEM((tm, d), F32)],
        compiler_params=_cparams(("parallel", "arbitrary")),
        name="ffn_swiglu",
    )(h, wg, wu, wd, x, g1, g2)


def _router_kernel(x_ref, g_ref, wr_ref, wts_ref, meta_ref, cnt_ref, carry_ref):
    @pl.when(pl.program_id(0) == 0)
    def _():
        carry_ref[...] = jnp.zeros_like(carry_ref)

    hn = _rms(x_ref[...], g_ref[...])
    tm = hn.shape[0]
    logits = jnp.dot(hn, wr_ref[...], preferred_element_type=F32, precision=lax.Precision.HIGHEST)
    lane = lax.broadcasted_iota(jnp.int32, (tm, LANES), 1)
    lanef = lane.astype(F32)
    lg = jnp.where(lane < N_EXPERTS, logits, -jnp.inf)
    v1 = jnp.max(lg, axis=1, keepdims=True)
    i1 = jnp.min(jnp.where(lg == v1, lanef, float(LANES)), axis=1, keepdims=True)
    lg2 = jnp.where(lanef == i1, -jnp.inf, lg)
    v2 = jnp.max(lg2, axis=1, keepdims=True)
    i2 = jnp.min(jnp.where(lg2 == v2, lanef, float(LANES)), axis=1, keepdims=True)
    e2 = jnp.exp(v2 - v1)
    den = 1.0 + e2
    wts_ref[...] = jnp.where(lane == 0, 1.0 / den, jnp.where(lane == 1, e2 / den, 0.0))

    onehot = jnp.where((lanef == i1) | (lanef == i2), 1.0, 0.0)
    tri = (lax.broadcasted_iota(jnp.int32, (tm, tm), 1) <= lax.broadcasted_iota(jnp.int32, (tm, tm), 0))
    incl = jnp.dot(jnp.where(tri, 1.0, 0.0).astype(MXU_DTYPE), onehot.astype(MXU_DTYPE),
                   preferred_element_type=F32)
    rank = carry_ref[...] + incl - onehot
    r1 = jnp.sum(jnp.where(lanef == i1, rank, 0.0), axis=1, keepdims=True)
    r2 = jnp.sum(jnp.where(lanef == i2, rank, 0.0), axis=1, keepdims=True)
    meta = jnp.where(lane == 0, i1, jnp.where(lane == 1, i2, jnp.where(lane == 2, r1,
                     jnp.where(lane == 3, r2, 0.0))))
    meta_ref[...] = meta.astype(jnp.int32)
    total = carry_ref[...] + incl[tm - 1:tm, :]
    carry_ref[...] = total
    cnt_ref[...] = total


def _router(x, g, wr, tm=512):
    m, d = x.shape
    row = lambda i: (i, 0)
    fix = lambda i: (0, 0)
    return pl.pallas_call(
        _router_kernel,
        out_shape=(jax.ShapeDtypeStruct((m, LANES), F32), jax.ShapeDtypeStruct((m, LANES), jnp.int32),
                   jax.ShapeDtypeStruct((1, LANES), F32)),
        grid=(m // tm,),
        in_specs=[pl.BlockSpec((tm, d), row), pl.BlockSpec((1, d), fix), pl.BlockSpec((d, LANES), fix)],
        out_specs=(pl.BlockSpec((tm, LANES), row), pl.BlockSpec((tm, LANES), row), pl.BlockSpec((1, LANES), fix)),
        scratch_shapes=[pltpu.VMEM((1, LANES), F32)],
        compiler_params=_cparams(("arbitrary",)),
        name="moe_router",
    )(x, g, wr)


def _row_copy(src_ref, s, dst_ref, d, sem):
    return pltpu.make_async_copy(src_ref.at[pl.ds(s, 1)], dst_ref.at[pl.ds(d, 1)], sem)


def _dispatch_kernel(dest_ref, x_ref, zeros_ref, xs_ref, sem, *, tc):
    del zeros_ref

    def issue(t, carry):
        _row_copy(x_ref, t, xs_ref, dest_ref[0, 0, 2 * t], sem).start()
        _row_copy(x_ref, t, xs_ref, dest_ref[0, 0, 2 * t + 1], sem).start()
        return carry

    lax.fori_loop(0, tc, issue, 0, unroll=8)

    def drain(t, carry):
        _row_copy(x_ref, 0, xs_ref, 0, sem).wait()
        _row_copy(x_ref, 0, xs_ref, 0, sem).wait()
        return carry

    lax.fori_loop(0, tc, drain, 0, unroll=8)


def _dispatch(x, dest, rows_total, tc=512):
    m, d = x.shape
    zeros = jnp.zeros((rows_total, d), x.dtype)
    return pl.pallas_call(
        functools.partial(_dispatch_kernel, tc=tc),
        out_shape=jax.ShapeDtypeStruct((rows_total, d), x.dtype),
        grid=(m // tc,),
        in_specs=[pl.BlockSpec((1, 1, 2 * tc), lambda i: (i, 0, 0), memory_space=pltpu.SMEM),
                  pl.BlockSpec((tc, d), lambda i: (i, 0)), pl.BlockSpec(memory_space=pl.ANY)],
        out_specs=pl.BlockSpec(memory_space=pl.ANY),
        scratch_shapes=[pltpu.SemaphoreType.DMA(())],
        input_output_aliases={2: 0},
        compiler_params=_cparams(("arbitrary",)),
        name="moe_dispatch",
    )(dest.reshape(m // tc, 1, 2 * tc), x, zeros)


def _experts_kernel(te_ref, tv_ref, xs_ref, g_ref, wg_ref, wu_ref, wd_ref, o_ref, hn_ref, acc_ref):
    n = pl.program_id(0)
    f = pl.program_id(1)
    live = tv_ref[n] > 0

    @pl.when(live & (f == 0))
    def _():
        hn_ref[...] = _rms(xs_ref[...], g_ref[...]).astype(hn_ref.dtype)
        acc_ref[...] = jnp.zeros_like(acc_ref)

    @pl.when(live)
    def _():
        h = hn_ref[...]
        a = jnp.dot(h, wg_ref[0], preferred_element_type=F32)
        u = jnp.dot(h, wu_ref[0], preferred_element_type=F32)
        acc_ref[...] += jnp.dot((jax.nn.silu(a) * u).astype(MXU_DTYPE), wd_ref[0], preferred_element_type=F32)

    last = f == pl.num_programs(1) - 1

    @pl.when(live & last)
    def _():
        o_ref[...] = acc_ref[...]

    @pl.when(jnp.logical_not(live) & last)
    def _():
        o_ref[...] = jnp.zeros_like(o_ref)


def _experts(xs, g, wg, wu, wd, tile_expert, tile_valid, tm, tf=1024):
    rows, d = xs.shape
    ff = wg.shape[2]
    nf = ff // tf

    def f_eff(n, f, tv):
        return jnp.where(tv[n] > 0, f, nf - 1)

    grid_spec = pltpu.PrefetchScalarGridSpec(
        num_scalar_prefetch=2,
        grid=(rows // tm, nf),
        in_specs=[pl.BlockSpec((tm, d), lambda n, f, te, tv: (n, 0)),
                  pl.BlockSpec((1, d), lambda n, f, te, tv: (0, 0)),
                  pl.BlockSpec((1, d, tf), lambda n, f, te, tv: (te[n], 0, f_eff(n, f, tv))),
                  pl.BlockSpec((1, d, tf), lambda n, f, te, tv: (te[n], 0, f_eff(n, f, tv))),
                  pl.BlockSpec((1, tf, d), lambda n, f, te, tv: (te[n], f_eff(n, f, tv), 0))],
        out_specs=pl.BlockSpec((tm, d), lambda n, f, te, tv: (n, 0)),
        scratch_shapes=[pltpu.VMEM((tm, d), MXU_DTYPE), pltpu.VMEM((tm, d), F32)],
    )
    return pl.pallas_call(
        _experts_kernel,
        out_shape=jax.ShapeDtypeStruct((rows, d), F32),
        grid_spec=grid_spec,
        compiler_params=_cparams(("arbitrary", "arbitrary")),
        name="moe_experts",
    )(tile_expert, tile_valid, xs, g, wg, wu, wd)


def _combine_kernel(dest_ref, y_ref, wts_ref, x_ref, g_ref, o_ref, buf_ref, sem, *, tm):
    def issue(t, carry):
        _row_copy(y_ref, dest_ref[0, 0, 2 * t], buf_ref.at[0], t, sem).start()
        _row_copy(y_ref, dest_ref[0, 0, 2 * t + 1], buf_ref.at[1], t, sem).start()
        return carry

    lax.fori_loop(0, tm, issue, 0, unroll=8)

    def drain(t, carry):
        _row_copy(y_ref, 0, buf_ref.at[0], 0, sem).wait()
        _row_copy(y_ref, 0, buf_ref.at[1], 0, sem).wait()
        return carry

    lax.fori_loop(0, tm, drain, 0, unroll=8)
    w = wts_ref[...]
    y = w[:, 0:1] * buf_ref[0] + w[:, 1:2] * buf_ref[1]
    o_ref[...] = x_ref[...] + _rms(y, g_ref[...])


def _combine(y, dest, wts, x, g, tm=256):
    m, d = x.shape
    row = lambda i: (i, 0)
    return pl.pallas_call(
        functools.partial(_combine_kernel, tm=tm),
        out_shape=jax.ShapeDtypeStruct((m, d), F32),
        grid=(m // tm,),
        in_specs=[pl.BlockSpec((1, 1, 2 * tm), lambda i: (i, 0, 0), memory_space=pltpu.SMEM),
                  pl.BlockSpec(memory_space=pl.ANY),
                  pl.BlockSpec((tm, LANES), row), pl.BlockSpec((tm, d), row),
                  pl.BlockSpec((1, d), lambda i: (0, 0))],
        out_specs=pl.BlockSpec((tm, d), row),
        scratch_shapes=[pltpu.VMEM((2, tm, d), F32), pltpu.SemaphoreType.DMA(())],
        compiler_params=_cparams(("arbitrary",)),
        name="moe_combine",
    )(dest.reshape(m // tm, 1, 2 * tm), y, wts, x, g)


def _rotary_tables(seq):
    pos = jnp.arange(seq, dtype=F32)
    inv = ROPE_THETA ** (-jnp.arange(0, ROT_DIM, 2, dtype=F32) / ROT_DIM)
    ang = pos[:, None] * inv[None, :]
    cos, sin = jnp.cos(ang), jnp.sin(ang)
    half = ROT_DIM // 2
    ones = jnp.ones((seq, HEAD_DIM - ROT_DIM), F32)
    zeros_h = jnp.zeros((seq, half), F32)
    zeros_t = jnp.zeros((seq, HEAD_DIM - ROT_DIM), F32)
    c = jnp.concatenate([cos, cos, ones], axis=1)
    s1 = jnp.concatenate([zeros_h, sin, zeros_t], axis=1)
    s2 = jnp.concatenate([-sin, zeros_h, zeros_t], axis=1)
    return c, s1, s2


def _pad_cols(w, n):
    return jnp.pad(w, ((0, 0), (0, n - w.shape[1])))


def _nsa_layer(x, hn, gains, w_in, pos_k, pos_v, wk1, wk2, wv1, wv2, w_out, batch, seq):
    qd, kvd = N_HEADS * HEAD_DIM, KV_GROUPS * HEAD_DIM
    cuts = [0, qd] + [qd + k * kvd for k in range(1, 7)] + [w_in.shape[1]]
    q_w, kc_w, vc_w, ksl_w, vsl_w, kw_w, vw_w, g_w = [w_in[:, a:b] for a, b in zip(cuts[:-1], cuts[1:])]
    tabs = _rotary_tables(seq)
    w_main = jnp.concatenate([q_w, ksl_w, vsl_w, kw_w, vw_w], axis=1).astype(MXU_DTYPE)
    nq_tiles = qd // PROJ_SUB
    qkv = _proj_rot(hn, w_main, tabs, seq, rot_tiles=tuple(range(nq_tiles)) + (nq_tiles, nq_tiles + 2),
                    scale_tiles=tuple(range(nq_tiles)), scale=Q_SCALE, out_dtype=MXU_DTYPE)
    w_c = jnp.concatenate([kc_w, vc_w], axis=1).astype(MXU_DTYPE)
    kcv = _proj_rot(hn, w_c, tabs, seq, rot_tiles=(0,), scale_tiles=(), scale=1.0, out_dtype=F32)
    gates = _proj_act(hn, _pad_cols(g_w, LANES).astype(MXU_DTYPE), jnp.zeros((1, LANES), F32),
                      act="sigmoid", out_dtype=F32)
    ck, cv = _compress(kcv, pos_k, pos_v, wk1.astype(MXU_DTYPE), wv1.astype(MXU_DTYPE),
                       wk2.astype(MXU_DTYPE), wv2.astype(MXU_DTYPE), batch, seq)
    attn = _nsa_attention(qkv, ck, cv, gates, batch, seq)
    return _outproj(attn, w_out.astype(MXU_DTYPE), x, gains[1:2], gains[2:3])


def _fox_layer(x, hn, gains, w_in, f_bias, w_out, batch, seq, tq=512):
    qkv_d = 3 * N_HEADS * HEAD_DIM
    nq_tiles = N_HEADS * HEAD_DIM // PROJ_SUB
    qkv = _proj_rot(hn, w_in[:, :qkv_d].astype(MXU_DTYPE), (), seq, rot_tiles=(),
                    scale_tiles=tuple(range(nq_tiles)), scale=Q_SCALE, out_dtype=MXU_DTYPE)
    bias = jnp.pad(f_bias.astype(F32), (0, LANES - N_HEADS)).reshape(1, LANES)
    log_f = _proj_act(hn, _pad_cols(w_in[:, qkv_d:], LANES).astype(MXU_DTYPE), bias,
                      act="log_sigmoid", out_dtype=F32)
    c_tok = _cumsum_rows(log_f, batch, seq)
    c_head = c_tok.reshape(batch, seq, LANES)[:, :, :N_HEADS].transpose(0, 2, 1).reshape(
        batch, N_HEADS, seq // tq, tq)
    attn = _fox_attention(qkv, c_tok, c_head, batch, seq, tq=tq)
    return _outproj(attn, w_out.astype(MXU_DTYPE), x, gains[1:2])


def _moe(x, gains, w_router, w_gate, w_up, w_down, tm=512):
    m, d = x.shape
    g_in, g_out = gains[2:3], gains[3:4]
    wts, meta, counts = _router(x, g_in, _pad_cols(w_router.astype(F32), LANES))
    i1, i2, r1, r2 = meta[:, 0], meta[:, 1], meta[:, 2], meta[:, 3]
    cnt = counts[0, :N_EXPERTS].astype(jnp.int32)
    padded = ((cnt + tm - 1) // tm) * tm
    ends = jnp.cumsum(padded)
    starts = ends - padded
    dest = jnp.stack([starts[i1] + r1, starts[i2] + r2], axis=1).reshape(-1).astype(jnp.int32)
    n_tiles = (2 * m) // tm + N_EXPERTS
    tile_start = jnp.arange(n_tiles, dtype=jnp.int32) * tm
    tile_valid = (tile_start < ends[-1]).astype(jnp.int32)
    last_live = jnp.maximum(ends[-1] // tm - 1, 0)
    tile_expert = jnp.sum(tile_start[:, None] >= ends[None, :], axis=1).astype(jnp.int32)
    tile_expert = jnp.where(tile_valid > 0, tile_expert, tile_expert[last_live]).astype(jnp.int32)
    tile_expert = jnp.minimum(tile_expert, N_EXPERTS - 1)
    xs = _dispatch(x, dest, n_tiles * tm)
    y = _experts(xs, g_in, w_gate.astype(MXU_DTYPE), w_up.astype(MXU_DTYPE), w_down.astype(MXU_DTYPE),
                 tile_expert, tile_valid, tm)
    return _combine(y, dest, wts, x, g_out)


def kernel(x, ln_gains, nsa_w_in, nsa_cmp_pos_k, nsa_cmp_pos_v, nsa_cmp_wk1, nsa_cmp_wk2, nsa_cmp_wv1, nsa_cmp_wv2,
           nsa_w_out, fox_w_in, fox_f_bias, fox_w_out, ffn_w_gate, ffn_w_up, ffn_w_down, moe_router, moe_w_gate,
           moe_w_up, moe_w_down):
    batch, seq, d = x.shape
    depth = ln_gains.shape[0]
    xf = x.reshape(batch * seq, d).astype(F32)
    hn = _rmsnorm(xf, ln_gains[0, 0:1])
    for i in range(depth):
        j = i // 2
        gains = ln_gains[i]
        last = i == depth - 1
        if i % 2 == 0:
            xf, hn2 = _nsa_layer(xf, hn, gains, nsa_w_in[j], nsa_cmp_pos_k[j], nsa_cmp_pos_v[j], nsa_cmp_wk1[j],
                                 nsa_cmp_wk2[j], nsa_cmp_wv1[j], nsa_cmp_wv2[j], nsa_w_out[j], batch, seq)
            g_next = ln_gains[i + 1, 0:1] if not last else gains[0:1]
            xf, hn = _ffn(hn2, ffn_w_gate[j].astype(MXU_DTYPE), ffn_w_up[j].astype(MXU_DTYPE),
                          ffn_w_down[j].astype(MXU_DTYPE), xf, gains[3:4], g_next)
        else:
            xf = _fox_layer(xf, hn, gains, fox_w_in[j], fox_f_bias[j], fox_w_out[j], batch, seq)
            xf = _moe(xf, gains, moe_router[j], moe_w_gate[j], moe_w_up[j], moe_w_down[j])
            if not last:
                hn = _rmsnorm(xf, ln_gains[i + 1, 0:1])
    return xf.reshape(batch, seq, d).astype(x.dtype)
```

```python
import functools
import math

import jax
import jax.numpy as jnp
from jax import lax
from jax.experimental import pallas as pl
from jax.experimental.pallas import tpu as pltpu

N_HEADS = 16
HEAD_DIM = 128
ROT_DIM = HEAD_DIM // 4
ROPE_THETA = 500000.0
KV_GROUPS = 4
HEADS_PER_GROUP = N_HEADS // KV_GROUPS
CMP_LEN = 32
CMP_STRIDE = 16
SLC_LEN = 64
N_SEL = 16
WINDOW = 512
FORCE_BONUS = 1000.0
N_EXPERTS = 8
RMS_EPS = 1e-6
NEG_INF = -1e30
LOG2E = math.log2(math.e)
Q_SCALE = LOG2E / math.sqrt(HEAD_DIM)
LANES = 128
MXU_DTYPE = jnp.bfloat16
VMEM_LIMIT = 56 * 1024 * 1024
F32 = jnp.float32

PROJ_SUB = 512
RMSNORM_TM = 512
PROJ_TM, PROJ_TN = 1024, 1024
NSA_TQ, NSA_TK = 256, 512
FOX_TQ, FOX_HPB = 512, 4
CUMSUM_TB = 512
OUTPROJ_TM = 512
FFN_TM, FFN_TF = 512, 512
ROUTER_TM = 512
MOE_TM, MOE_TF = 512, 1024
DISPATCH_TC = 512
COMBINE_TM = 512

_NT = (((1,), (1,)), ((), ()))


def _cparams(sem):
    return pltpu.CompilerParams(dimension_semantics=sem, vmem_limit_bytes=VMEM_LIMIT)


def _rms(x, g):
    ms = jnp.mean(x * x, axis=-1, keepdims=True)
    return x * lax.rsqrt(ms + RMS_EPS) * g


def _split3(x):
    a = x.astype(MXU_DTYPE)
    r = x - a.astype(F32)
    b = r.astype(MXU_DTYPE)
    c = (r - b.astype(F32)).astype(MXU_DTYPE)
    return a, b, c


def _rmsnorm_kernel(x_ref, g_ref, o_ref):
    o_ref[...] = _rms(x_ref[...], g_ref[...]).astype(o_ref.dtype)


def _rmsnorm(x, g, tm=RMSNORM_TM):
    m, d = x.shape
    return pl.pallas_call(
        _rmsnorm_kernel,
        out_shape=jax.ShapeDtypeStruct((m, d), MXU_DTYPE),
        grid=(m // tm,),
        in_specs=[pl.BlockSpec((tm, d), lambda i: (i, 0)), pl.BlockSpec((1, d), lambda i: (0, 0))],
        out_specs=pl.BlockSpec((tm, d), lambda i: (i, 0)),
        compiler_params=_cparams(("parallel",)),
        name="rmsnorm",
    )(x, g)


def _any_tile(j, tiles):
    return functools.reduce(jnp.logical_or, [j == t for t in tiles])


def _proj_rot_kernel(a_ref, w_ref, *rest, rot_tiles, scale_tiles, scale):
    o_ref = rest[-1]
    j = pl.program_id(1)
    a = a_ref[...]
    nsub = o_ref.shape[1] // PROJ_SUB
    for cb in range(nsub):
        cols = slice(cb * PROJ_SUB, (cb + 1) * PROJ_SUB)
        blk = j * nsub + cb
        out = jnp.dot(a, w_ref[:, cols], preferred_element_type=F32)
        if rot_tiles:
            c_ref, s1_ref, s2_ref = rest[:3]
            rotf = jnp.where(_any_tile(blk, rot_tiles), 1.0, 0.0).astype(F32)
            reps = PROJ_SUB // HEAD_DIM
            c = jnp.concatenate([1.0 + rotf * (c_ref[...] - 1.0)] * reps, axis=1)
            s1 = jnp.concatenate([rotf * s1_ref[...]] * reps, axis=1)
            s2 = jnp.concatenate([rotf * s2_ref[...]] * reps, axis=1)
            half = ROT_DIM // 2
            out = out * c + pltpu.roll(out, half, 1) * s1 + pltpu.roll(out, PROJ_SUB - half, 1) * s2
        if scale_tiles:
            out = out * jnp.where(_any_tile(blk, scale_tiles), scale, 1.0).astype(F32)
        o_ref[:, cols] = out.astype(o_ref.dtype)


def _proj_rot(a, w, tabs, seq, *, rot_tiles, scale_tiles, scale, out_dtype, tm=PROJ_TM, tn=PROJ_TN):
    m, k = a.shape
    n = w.shape[1]
    nrow = seq // tm
    tab_spec = pl.BlockSpec((tm, HEAD_DIM), lambda i, j: (i % nrow, 0))
    tabs = tuple(tabs) if rot_tiles else ()
    return pl.pallas_call(
        functools.partial(_proj_rot_kernel, rot_tiles=rot_tiles, scale_tiles=scale_tiles, scale=scale),
        out_shape=jax.ShapeDtypeStruct((m, n), out_dtype),
        grid=(m // tm, n // tn),
        in_specs=[pl.BlockSpec((tm, k), lambda i, j: (i, 0)),
                  pl.BlockSpec((k, tn), lambda i, j: (0, j))] + [tab_spec] * len(tabs),
        out_specs=pl.BlockSpec((tm, tn), lambda i, j: (i, j)),
        compiler_params=_cparams(("parallel", "arbitrary")),
        name="proj_rot" if rot_tiles else "proj_plain",
    )(a, w, *tabs)


def _proj_act_kernel(a_ref, w_ref, b_ref, o_ref, *, act):
    acc = jnp.dot(a_ref[...], w_ref[...], preferred_element_type=F32) + b_ref[...]
    if act == "sigmoid":
        acc = jax.nn.sigmoid(acc)
    elif act == "log_sigmoid":
        acc = jax.nn.log_sigmoid(acc)
    o_ref[...] = acc.astype(o_ref.dtype)


def _proj_act(a, w, b, *, act, out_dtype, tm=PROJ_TM, tn=PROJ_SUB):
    m, k = a.shape
    n = w.shape[1]
    tn = min(tn, n)
    return pl.pallas_call(
        functools.partial(_proj_act_kernel, act=act),
        out_shape=jax.ShapeDtypeStruct((m, n), out_dtype),
        grid=(m // tm, n // tn),
        in_specs=[pl.BlockSpec((tm, k), lambda i, j: (i, 0)),
                  pl.BlockSpec((k, tn), lambda i, j: (0, j)),
                  pl.BlockSpec((1, tn), lambda i, j: (0, j))],
        out_specs=pl.BlockSpec((tm, tn), lambda i, j: (i, j)),
        compiler_params=_cparams(("parallel", "arbitrary")),
        name="proj_act",
    )(a, w, b)


def _compress_kernel(x_ref, posk_ref, posv_ref, wka_ref, wkb_ref, wva_ref, wvb_ref, w2k_ref, w2v_ref,
                     ck_ref, cv_ref, acc_ref):
    l = pl.program_id(1)
    nl = pl.num_programs(1)

    @pl.when(l == 0)
    def _():
        acc_ref[...] = jnp.zeros_like(acc_ref)

    x = x_ref[0]
    pos = (posk_ref, posv_ref)
    wa = (wka_ref, wva_ref)
    wb = (wkb_ref, wvb_ref)
    for kv in range(2):
        pa = pos[kv][pl.ds(l, 1), :]
        pb = pos[kv][pl.ds(l + CMP_STRIDE, 1), :]
        for g in range(KV_GROUPS):
            off = (kv * KV_GROUPS + g) * HEAD_DIM
            xg = x[:, off:off + HEAD_DIM]
            acc_ref[kv, g, 0] += jnp.dot((xg + pa).astype(MXU_DTYPE), wa[kv][...], preferred_element_type=F32)
            acc_ref[kv, g, 1] += jnp.dot((xg + pb).astype(MXU_DTYPE), wb[kv][...], preferred_element_type=F32)

    @pl.when(l == nl - 1)
    def _():
        rows = acc_ref.shape[3]
        w2 = (w2k_ref, w2v_ref)
        outs = (ck_ref, cv_ref)
        for kv in range(2):
            for g in range(KV_GROUPS):
                nxt = pltpu.roll(acc_ref[kv, g, 1], rows - 1, 0)
                h = jax.nn.gelu(acc_ref[kv, g, 0] + nxt)
                outs[kv][0, g] = jnp.dot(h.astype(MXU_DTYPE), w2[kv][...],
                                         preferred_element_type=F32).astype(outs[kv].dtype)


def _compress(kcv, pos_k, pos_v, w1k, w1v, w2k, w2v, batch, seq):
    chunks = seq // CMP_STRIDE
    width = 2 * KV_GROUPS * HEAD_DIM
    x = kcv.reshape(batch, chunks, CMP_STRIDE * width)
    hid = w1k.shape[1]
    wa_spec = pl.BlockSpec((HEAD_DIM, hid), lambda b, l: (l, 0))
    wb_spec = pl.BlockSpec((HEAD_DIM, hid), lambda b, l: (l + CMP_STRIDE, 0))
    full = lambda shp: pl.BlockSpec(shp, lambda b, l: (0,) * len(shp))
    out_sds = jax.ShapeDtypeStruct((batch, KV_GROUPS, chunks, HEAD_DIM), MXU_DTYPE)
    out_spec = pl.BlockSpec((1, KV_GROUPS, chunks, HEAD_DIM), lambda b, l: (b, 0, 0, 0))
    return pl.pallas_call(
        _compress_kernel,
        out_shape=(out_sds, out_sds),
        grid=(batch, CMP_STRIDE),
        in_specs=[pl.BlockSpec((1, chunks, width), lambda b, l: (b, 0, l)),
                  full(pos_k.shape), full(pos_v.shape),
                  wa_spec, wb_spec, wa_spec, wb_spec,
                  full(w2k.shape), full(w2v.shape)],
        out_specs=(out_spec, out_spec),
        scratch_shapes=[pltpu.VMEM((2, KV_GROUPS, 2, chunks, hid), F32)],
        compiler_params=_cparams(("parallel", "arbitrary")),
        name="nsa_compress",
    )(x, pos_k, pos_v, w1k, w1k, w1v, w1v, w2k, w2v)


def _scores(q, kb, *, bias=None, col_sub=None):
    s = lax.dot_general(q, kb, _NT, preferred_element_type=F32)
    if col_sub is not None:
        s = s - col_sub
    if bias is not None:
        rows, tk = s.shape
        br = bias.shape[0]
        s = (s.reshape(rows // br, br, tk) + bias[None]).reshape(rows, tk)
    return s


def _softmax_pv(s, vb, m_scr, l_scr, acc_scr, row_shift=None):
    reps = s.shape[1] // LANES
    m_prev = m_scr[...]
    m_cur = jnp.max(s, axis=1, keepdims=True)
    if row_shift is not None:
        m_cur = m_cur + row_shift
    m_new = jnp.maximum(m_prev, m_cur)
    alpha = jnp.exp2(m_prev - m_new)
    off = m_new if row_shift is None else m_new - row_shift
    p = jnp.exp2(s - jnp.concatenate([off] * reps, axis=1))
    l_scr[...] = alpha * l_scr[...] + jnp.sum(p, axis=1, keepdims=True)
    acc_scr[...] = alpha * acc_scr[...] + jnp.dot(p.astype(MXU_DTYPE), vb, preferred_element_type=F32)
    m_scr[...] = m_new


def _nsa_attn_kernel(q_ref, ksl_ref, vsl_ref, kw_ref, vw_ref, ck_ref, cv_ref, gate_ref, ovl_ref, exp_ref,
                     o_ref, m_scr, l_scr, acc_scr, win_scr, *, tq, tk, ns, n_sel):
    g = pl.program_id(1)
    qi = pl.program_id(2)
    q0 = qi * tq
    rph = HEADS_PER_GROUP
    rows = rph * tq
    qb = q_ref[...]
    qrows = jnp.concatenate([qb[:, r * HEAD_DIM:(r + 1) * HEAD_DIM] for r in range(rph)], axis=0)
    qpos = q0 + lax.broadcasted_iota(jnp.int32, (tq, 1), 0)

    wlen = tq + WINDOW
    st = pl.multiple_of(jnp.maximum(q0 - WINDOW, 0), tq)
    kwb = kw_ref[pl.ds(st, wlen), :]
    vwb = vw_ref[pl.ds(st, wlen), :]
    sw = lax.dot_general(qrows, kwb, _NT, preferred_element_type=F32).reshape(rph, tq, wlen)
    kpos = st + lax.broadcasted_iota(jnp.int32, (tq, wlen), 1)
    okw = (kpos <= qpos) & (kpos > qpos - WINDOW)
    sw = sw + jnp.where(okw, 0.0, NEG_INF)[None]
    pw = jnp.exp2(sw - jnp.max(sw, axis=-1, keepdims=True))
    lw = jnp.maximum(jnp.sum(pw, axis=-1, keepdims=True), 1e-30).reshape(rows, 1)
    win_scr[...] = jnp.dot(pw.reshape(rows, wlen).astype(MXU_DTYPE), vwb, preferred_element_type=F32) * (1.0 / lw)

    ck = ck_ref[0, 0]
    ncp = ck.shape[0]
    s = lax.dot_general(qrows, ck, _NT, preferred_element_type=F32).reshape(rph, tq, ncp)
    cidx = lax.broadcasted_iota(jnp.int32, (tq, ncp), 1)
    valid = cidx * CMP_STRIDE + (CMP_LEN - 1) <= qpos
    s = s + jnp.where(valid, 0.0, NEG_INF)[None]
    e = jnp.exp2(s - jnp.max(s, axis=-1, keepdims=True))
    den = jnp.maximum(jnp.sum(e, axis=-1, keepdims=True), 1e-30)
    any_valid = (qpos >= CMP_LEN - 1)[None]
    p = e * jnp.where(any_valid, 1.0 / den, 0.0)

    psum = p[0]
    for r in range(1, rph):
        psum = psum + p[r]
    hi = psum.astype(MXU_DTYPE)
    lo = (psum - hi.astype(F32)).astype(MXU_DTYPE)
    ovl = ovl_ref[...]
    imp = jnp.dot(hi, ovl, preferred_element_type=F32) + jnp.dot(lo, ovl, preferred_element_type=F32)
    o_cmp = jnp.dot(p.reshape(rows, ncp).astype(MXU_DTYPE), cv_ref[0, 0], preferred_element_type=F32)
    nidx = lax.broadcasted_iota(jnp.int32, (tq, LANES), 1)
    jq = lax.shift_right_logical(qpos, int(math.log2(SLC_LEN)))
    forced = (nidx == 0) | (nidx == jq) | (nidx == jq - 1)
    score = jnp.where(nidx <= jq, imp + jnp.where(forced, FORCE_BONUS, 0.0), -1.0)
    score = jnp.where(nidx < ns, score, -2.0)
    s_t = score.T
    sub = 8
    nsp = ((ns + sub - 1) // sub) * sub
    cands = [s_t[j:j + sub] for j in range(0, nsp, sub)]
    sub_io = lax.broadcasted_iota(jnp.int32, (sub, tq), 0)
    cnts = [jnp.zeros((sub, tq), jnp.int32) for _ in cands]
    for mm in range(ns):
        row = s_t[mm:mm + 1, :]
        for jb, cand in enumerate(cands):
            lo_n = jb * sub
            if lo_n > mm:
                beats = row >= cand
            elif lo_n + sub - 1 <= mm:
                beats = row > cand
            else:
                beats = (row > cand) | ((row == cand) & (sub_io > mm - lo_n))
            cnts[jb] = cnts[jb] + jnp.where(beats, 1, 0)
    cnt = jnp.concatenate(cnts, axis=0)
    sel_t = jnp.where(cnt < n_sel, 1.0, 0.0).astype(F32)
    if nsp < LANES:
        sel_t = jnp.concatenate([sel_t, jnp.zeros((LANES - nsp, tq), F32)], axis=0)
    sel = sel_t.T.astype(MXU_DTYPE)

    m_scr[...] = jnp.full_like(m_scr, NEG_INF)
    l_scr[...] = jnp.zeros_like(l_scr)
    acc_scr[...] = jnp.zeros_like(acc_scr)

    def slc_body(kt, carry):
        k0 = pl.multiple_of(kt * tk, tk)
        kb = ksl_ref[pl.ds(k0, tk), :]
        vb = vsl_ref[pl.ds(k0, tk), :]
        selx = jnp.dot(sel, exp_ref[kt], preferred_element_type=F32)
        kpos = k0 + lax.broadcasted_iota(jnp.int32, (tq, tk), 1)
        bias = jnp.where((selx > 0.5) & (kpos <= qpos), 0.0, NEG_INF)
        _softmax_pv(_scores(qrows, kb, bias=bias), vb, m_scr, l_scr, acc_scr)
        return carry

    lax.fori_loop(0, (q0 + tq + tk - 1) // tk, slc_body, 0)
    o_slc = acc_scr[...] / jnp.maximum(l_scr[...], 1e-30)
    o_win = win_scr[...]

    gt = gate_ref[...]
    outs = []
    for r in range(rph):
        sl = slice(r * tq, (r + 1) * tq)
        acc = None
        for br, ob in enumerate((o_cmp, o_slc, o_win)):
            lane_id = br * N_HEADS + g * rph + r
            gcol = jnp.sum(jnp.where(nidx == lane_id, gt, 0.0), axis=1, keepdims=True)
            term = gcol * ob[sl]
            acc = term if acc is None else acc + term
        outs.append(acc)
    o_ref[...] = jnp.concatenate(outs, axis=1).astype(o_ref.dtype)


def _nsa_attention(qkv, ck, cv, gates, batch, seq, tq=NSA_TQ, tk=NSA_TK):
    t = batch * seq
    nq = seq // tq
    ns = seq // SLC_LEN
    n_sel = min(N_SEL, ns)
    ncp = ck.shape[2]
    gw = HEADS_PER_GROUP * HEAD_DIM
    c = jnp.arange(ncp)
    n = jnp.arange(LANES)
    cs, ce = c * CMP_STRIDE, c * CMP_STRIDE + CMP_LEN - 1
    nc = (seq - CMP_LEN) // CMP_STRIDE + 1
    ovl = ((cs[:, None] < n[None, :] * SLC_LEN + SLC_LEN) & (ce[:, None] >= n[None, :] * SLC_LEN)
           & (c[:, None] < nc) & (n[None, :] < ns)).astype(MXU_DTYPE)
    kp = jnp.arange(seq).reshape(seq // tk, 1, tk)
    expand = (kp // SLC_LEN == n[None, :, None]).astype(MXU_DTYPE)
    kv_col0 = N_HEADS

    def kv_spec(which):
        return pl.BlockSpec((seq, HEAD_DIM), lambda b, g, i: (b, kv_col0 + which * KV_GROUPS + g))

    cmp_spec = pl.BlockSpec((1, 1, ncp, HEAD_DIM), lambda b, g, i: (b, g, 0, 0))
    rows = HEADS_PER_GROUP * tq
    return pl.pallas_call(
        functools.partial(_nsa_attn_kernel, tq=tq, tk=tk, ns=ns, n_sel=n_sel),
        out_shape=jax.ShapeDtypeStruct((t, N_HEADS * HEAD_DIM), MXU_DTYPE),
        grid=(batch, KV_GROUPS, nq),
        in_specs=[pl.BlockSpec((tq, gw), lambda b, g, i: (b * nq + i, g)),
                  kv_spec(0), kv_spec(1), kv_spec(2), kv_spec(3),
                  cmp_spec, cmp_spec,
                  pl.BlockSpec((tq, LANES), lambda b, g, i: (b * nq + i, 0)),
                  pl.BlockSpec(ovl.shape, lambda b, g, i: (0, 0)),
                  pl.BlockSpec(expand.shape, lambda b, g, i: (0, 0, 0))],
        out_specs=pl.BlockSpec((tq, gw), lambda b, g, i: (b * nq + i, g)),
        scratch_shapes=[pltpu.VMEM((rows, LANES), F32), pltpu.VMEM((rows, LANES), F32),
                        pltpu.VMEM((rows, HEAD_DIM), F32), pltpu.VMEM((rows, HEAD_DIM), F32)],
        compiler_params=_cparams(("parallel", "parallel", "arbitrary")),
        name="nsa_attention",
    )(qkv, qkv, qkv, qkv, qkv, ck, cv, gates, ovl, expand)


def _cumsum_kernel(x_ref, o_ref, carry_ref):
    @pl.when(pl.program_id(1) == 0)
    def _():
        carry_ref[...] = jnp.zeros_like(carry_ref)

    x = x_ref[...]
    n = x.shape[0]
    tri = (lax.broadcasted_iota(jnp.int32, (n, n), 1) <= lax.broadcasted_iota(jnp.int32, (n, n), 0))
    tri = jnp.where(tri, 1.0, 0.0).astype(MXU_DTYPE)
    a, b, c = _split3(x)
    inc = (jnp.dot(tri, a, preferred_element_type=F32) + jnp.dot(tri, b, preferred_element_type=F32)
           + jnp.dot(tri, c, preferred_element_type=F32))
    out = inc + carry_ref[...]
    o_ref[...] = out
    carry_ref[...] = out[n - 1:n, :]


def _cumsum_rows(x, batch, seq, tb=CUMSUM_TB):
    nb = seq // tb
    return pl.pallas_call(
        _cumsum_kernel,
        out_shape=jax.ShapeDtypeStruct(x.shape, F32),
        grid=(batch, nb),
        in_specs=[pl.BlockSpec((tb, x.shape[1]), lambda b, i: (b * nb + i, 0))],
        out_specs=pl.BlockSpec((tb, x.shape[1]), lambda b, i: (b * nb + i, 0)),
        scratch_shapes=[pltpu.VMEM((1, x.shape[1]), F32)],
        compiler_params=_cparams(("parallel", "arbitrary")),
        name="fox_cumsum",
    )(x)


def _fox_attn_kernel(q_ref, k_ref, v_ref, cq_ref, ck_ref, o_ref, m_scr, l_scr, acc_scr, *, tq, hpb):
    hb = pl.program_id(1)
    qi = pl.program_id(2)
    lane = lax.broadcasted_iota(jnp.int32, (tq, LANES), 1)
    cq_all = cq_ref[...]
    hd = lambda j: slice(j * HEAD_DIM, (j + 1) * HEAD_DIM)
    qs = [q_ref[:, hd(j)] for j in range(hpb)]
    cqs = [jnp.sum(jnp.where(lane == hb * hpb + j, cq_all, 0.0), axis=1, keepdims=True) * LOG2E for j in range(hpb)]
    m_scr[...] = jnp.full_like(m_scr, NEG_INF)
    l_scr[...] = jnp.zeros_like(l_scr)
    acc_scr[...] = jnp.zeros_like(acc_scr)

    def step(kt, causal):
        k0 = pl.multiple_of(kt * tq, tq)
        sc = lambda j: _scores(qs[j], k_ref[pl.ds(k0, tq), hd(j)], col_sub=ck_ref[0, j, pl.ds(kt, 1), :] * LOG2E)
        s_next = sc(0)
        for j in range(hpb):
            s = s_next
            if j + 1 < hpb:
                s_next = sc(j + 1)
            if causal is not None:
                s = jnp.where(causal, s, NEG_INF)
            _softmax_pv(s, v_ref[pl.ds(k0, tq), hd(j)], m_scr.at[j], l_scr.at[j], acc_scr.at[j], row_shift=cqs[j])

    def body(kt, carry):
        step(kt, None)
        return carry

    lax.fori_loop(0, qi, body, 0)
    step(qi, lax.broadcasted_iota(jnp.int32, (tq, tq), 1) <= lax.broadcasted_iota(jnp.int32, (tq, tq), 0))
    for j in range(hpb):
        o_ref[:, hd(j)] = (acc_scr[j] / jnp.maximum(l_scr[j], 1e-30)).astype(o_ref.dtype)


def _fox_attention(qkv, c_tok, c_head, batch, seq, tq=FOX_TQ, hpb=FOX_HPB):
    t = batch * seq
    nq = seq // tq
    nhb = N_HEADS // hpb
    w = hpb * HEAD_DIM
    return pl.pallas_call(
        functools.partial(_fox_attn_kernel, tq=tq, hpb=hpb),
        out_shape=jax.ShapeDtypeStruct((t, N_HEADS * HEAD_DIM), MXU_DTYPE),
        grid=(batch, nhb, nq),
        in_specs=[pl.BlockSpec((tq, w), lambda b, h, i: (b * nq + i, h)),
                  pl.BlockSpec((seq, w), lambda b, h, i: (b, nhb + h)),
                  pl.BlockSpec((seq, w), lambda b, h, i: (b, 2 * nhb + h)),
                  pl.BlockSpec((tq, LANES), lambda b, h, i: (b * nq + i, 0)),
                  pl.BlockSpec((1, hpb, nq, tq), lambda b, h, i: (b, h, 0, 0))],
        out_specs=pl.BlockSpec((tq, w), lambda b, h, i: (b * nq + i, h)),
        scratch_shapes=[pltpu.VMEM((hpb, tq, LANES), F32), pltpu.VMEM((hpb, tq, LANES), F32),
                        pltpu.VMEM((hpb, tq, HEAD_DIM), F32)],
        compiler_params=_cparams(("parallel", "parallel", "arbitrary")),
        name="fox_attention",
    )(qkv, qkv, qkv, c_tok, c_head)


def _outproj_kernel(a_ref, w_ref, x_ref, g1_ref, *rest):
    y = jnp.dot(a_ref[...], w_ref[...], preferred_element_type=F32)
    xn = x_ref[...] + _rms(y, g1_ref[...])
    if len(rest) == 1:
        rest[0][...] = xn
    else:
        g2_ref, xo_ref, hn_ref = rest
        xo_ref[...] = xn
        hn_ref[...] = _rms(xn, g2_ref[...]).astype(hn_ref.dtype)


def _outproj(a, w, x, g1, g2=None, tm=OUTPROJ_TM):
    m, k = a.shape
    d = w.shape[1]
    row = lambda i: (i, 0)
    fix = lambda i: (0, 0)
    in_specs = [pl.BlockSpec((tm, k), row), pl.BlockSpec((k, d), fix), pl.BlockSpec((tm, d), row),
                pl.BlockSpec((1, d), fix)]
    x_sds, x_spec = jax.ShapeDtypeStruct((m, d), F32), pl.BlockSpec((tm, d), row)
    if g2 is None:
        args, out_shape, out_specs = (a, w, x, g1), x_sds, x_spec
    else:
        args = (a, w, x, g1, g2)
        in_specs = in_specs + [pl.BlockSpec((1, d), fix)]
        out_shape = (x_sds, jax.ShapeDtypeStruct((m, d), MXU_DTYPE))
        out_specs = (x_spec, pl.BlockSpec((tm, d), row))
    return pl.pallas_call(
        _outproj_kernel,
        out_shape=out_shape,
        grid=(m // tm,),
        in_specs=in_specs,
        out_specs=out_specs,
        compiler_params=_cparams(("parallel",)),
        name="outproj_norm",
    )(*args)


def _ffn_kernel(h_ref, wg_ref, wu_ref, wd_ref, x_ref, g1_ref, g2_ref, xo_ref, hn_ref, acc_ref):
    f = pl.program_id(1)

    @pl.when(f == 0)
    def _():
        acc_ref[...] = jnp.zeros_like(acc_ref)

    h = h_ref[...]
    a = jnp.dot(h, wg_ref[...], preferred_element_type=F32)
    u = jnp.dot(h, wu_ref[...], preferred_element_type=F32)
    acc_ref[...] += jnp.dot((jax.nn.silu(a) * u).astype(MXU_DTYPE), wd_ref[...], preferred_element_type=F32)

    @pl.when(f == pl.num_programs(1) - 1)
    def _():
        xn = x_ref[...] + _rms(acc_ref[...], g1_ref[...])
        xo_ref[...] = xn
        hn_ref[...] = _rms(xn, g2_ref[...]).astype(hn_ref.dtype)


def _ffn(h, wg, wu, wd, x, g1, g2, tm=FFN_TM, tf=FFN_TF):
    m, d = h.shape
    ff = wg.shape[1]
    row = lambda i, f: (i, 0)
    fix = lambda i, f: (0, 0)
    return pl.pallas_call(
        _ffn_kernel,
        out_shape=(jax.ShapeDtypeStruct((m, d), F32), jax.ShapeDtypeStruct((m, d), MXU_DTYPE)),
        grid=(m // tm, ff // tf),
        in_specs=[pl.BlockSpec((tm, d), row),
                  pl.BlockSpec((d, tf), lambda i, f: (0, f)),
                  pl.BlockSpec((d, tf), lambda i, f: (0, f)),
                  pl.BlockSpec((tf, d), lambda i, f: (f, 0)),
                  pl.BlockSpec((tm, d), row), pl.BlockSpec((1, d), fix), pl.BlockSpec((1, d), fix)],
        out_specs=(pl.BlockSpec((tm, d), row), pl.BlockSpec((tm, d), row)),
        scratch_shapes=[pltpu.VMEM((tm, d), F32)],
        compiler_params=_cparams(("parallel", "arbitrary")),
        name="ffn_swiglu",
    )(h, wg, wu, wd, x, g1, g2)


def _router_kernel(x_ref, g_ref, wr_ref, wts_ref, meta_ref, cnt_ref, carry_ref):
    @pl.when(pl.program_id(0) == 0)
    def _():
        carry_ref[...] = jnp.zeros_like(carry_ref)

    hn = _rms(x_ref[...], g_ref[...])
    tm = hn.shape[0]
    logits = jnp.dot(hn, wr_ref[...], preferred_element_type=F32, precision=lax.Precision.HIGHEST)
    lane = lax.broadcasted_iota(jnp.int32, (tm, LANES), 1)
    lanef = lane.astype(F32)
    lg = jnp.where(lane < N_EXPERTS, logits, -jnp.inf)
    v1 = jnp.max(lg, axis=1, keepdims=True)
    i1 = jnp.min(jnp.where(lg == v1, lanef, float(LANES)), axis=1, keepdims=True)
    lg2 = jnp.where(lanef == i1, -jnp.inf, lg)
    v2 = jnp.max(lg2, axis=1, keepdims=True)
    i2 = jnp.min(jnp.where(lg2 == v2, lanef, float(LANES)), axis=1, keepdims=True)
    e2 = jnp.exp(v2 - v1)
    den = 1.0 + e2
    wts_ref[...] = jnp.where(lane == 0, 1.0 / den, jnp.where(lane == 1, e2 / den, 0.0))

    onehot = jnp.where((lanef == i1) | (lanef == i2), 1.0, 0.0)
    tri = (lax.broadcasted_iota(jnp.int32, (tm, tm), 1) <= lax.broadcasted_iota(jnp.int32, (tm, tm), 0))
    incl = jnp.dot(jnp.where(tri, 1.0, 0.0).astype(MXU_DTYPE), onehot.astype(MXU_DTYPE),
                   preferred_element_type=F32)
    rank = carry_ref[...] + incl - onehot
    r1 = jnp.sum(jnp.where(lanef == i1, rank, 0.0), axis=1, keepdims=True)
    r2 = jnp.sum(jnp.where(lanef == i2, rank, 0.0), axis=1, keepdims=True)
    meta = jnp.where(lane == 0, i1, jnp.where(lane == 1, i2, jnp.where(lane == 2, r1,
                     jnp.where(lane == 3, r2, 0.0))))
    meta_ref[...] = meta.astype(jnp.int32)
    total = carry_ref[...] + incl[tm - 1:tm, :]
    carry_ref[...] = total
    cnt_ref[...] = total


def _router(x, g, wr, tm=ROUTER_TM):
    m, d = x.shape
    row = lambda i: (i, 0)
    fix = lambda i: (0, 0)
    return pl.pallas_call(
        _router_kernel,
        out_shape=(jax.ShapeDtypeStruct((m, LANES), F32), jax.ShapeDtypeStruct((m, LANES), jnp.int32),
                   jax.ShapeDtypeStruct((1, LANES), F32)),
        grid=(m // tm,),
        in_specs=[pl.BlockSpec((tm, d), row), pl.BlockSpec((1, d), fix), pl.BlockSpec((d, LANES), fix)],
        out_specs=(pl.BlockSpec((tm, LANES), row), pl.BlockSpec((tm, LANES), row), pl.BlockSpec((1, LANES), fix)),
        scratch_shapes=[pltpu.VMEM((1, LANES), F32)],
        compiler_params=_cparams(("arbitrary",)),
        name="moe_router",
    )(x, g, wr)


def _row_copy(src_ref, s, dst_ref, d, sem):
    return pltpu.make_async_copy(src_ref.at[pl.ds(s, 1)], dst_ref.at[pl.ds(d, 1)], sem)


def _dispatch_kernel(dest_ref, pad_ref, x_ref, xs_ref, sem, *, tc):
    def issue(t, carry):
        _row_copy(x_ref, t, xs_ref, dest_ref[0, 0, 2 * t], sem).start()
        _row_copy(x_ref, t, xs_ref, dest_ref[0, 0, 2 * t + 1], sem).start()
        return carry

    lax.fori_loop(0, tc, issue, 0, unroll=8)

    def drain(t, carry):
        _row_copy(x_ref, 0, xs_ref, 0, sem).wait()
        _row_copy(x_ref, 0, xs_ref, 0, sem).wait()
        return carry

    lax.fori_loop(0, tc, drain, 0, unroll=8)

    @pl.when(pl.program_id(0) == pl.num_programs(0) - 1)
    def _():
        n_pad = pad_ref.shape[1]

        def issue_pad(p, carry):
            _row_copy(x_ref, 0, xs_ref, pad_ref[0, p], sem).start()
            return carry

        lax.fori_loop(0, n_pad, issue_pad, 0, unroll=8)

        def drain_pad(p, carry):
            _row_copy(x_ref, 0, xs_ref, 0, sem).wait()
            return carry

        lax.fori_loop(0, n_pad, drain_pad, 0, unroll=8)


def _dispatch(x, dest, pad_dst, rows_total, tc=DISPATCH_TC):
    m, d = x.shape
    return pl.pallas_call(
        functools.partial(_dispatch_kernel, tc=tc),
        out_shape=jax.ShapeDtypeStruct((rows_total, d), x.dtype),
        grid=(m // tc,),
        in_specs=[pl.BlockSpec((1, 1, 2 * tc), lambda i: (i, 0, 0), memory_space=pltpu.SMEM),
                  pl.BlockSpec(pad_dst.shape, lambda i: (0, 0), memory_space=pltpu.SMEM),
                  pl.BlockSpec((tc, d), lambda i: (i, 0))],
        out_specs=pl.BlockSpec(memory_space=pl.ANY),
        scratch_shapes=[pltpu.SemaphoreType.DMA(())],
        compiler_params=_cparams(("arbitrary",)),
        name="moe_dispatch",
    )(dest.reshape(m // tc, 1, 2 * tc), pad_dst, x)


def _experts_kernel(te_ref, tv_ref, xs_ref, g_ref, wg_ref, wu_ref, wd_ref, o_ref, hn_ref, acc_ref):
    n = pl.program_id(0)
    f = pl.program_id(1)
    live = tv_ref[n] == n

    @pl.when(live & (f == 0))
    def _():
        hn_ref[...] = _rms(xs_ref[...], g_ref[...]).astype(hn_ref.dtype)
        acc_ref[...] = jnp.zeros_like(acc_ref)

    @pl.when(live)
    def _():
        h = hn_ref[...]
        a = jnp.dot(h, wg_ref[0], preferred_element_type=F32)
        u = jnp.dot(h, wu_ref[0], preferred_element_type=F32)
        acc_ref[...] += jnp.dot((jax.nn.silu(a) * u).astype(MXU_DTYPE), wd_ref[0], preferred_element_type=F32)

    last = f == pl.num_programs(1) - 1

    @pl.when(live & last)
    def _():
        o_ref[...] = acc_ref[...]

    @pl.when(jnp.logical_not(live) & last)
    def _():
        o_ref[...] = jnp.zeros_like(o_ref)


def _experts(xs, g, wg, wu, wd, tile_expert, tile_row, tm, tf=MOE_TF):
    n_tiles = tile_row.shape[0]
    rows, d = n_tiles * tm, xs.shape[1]
    ff = wg.shape[2]
    nf = ff // tf

    def f_eff(n, f, tv):
        return jnp.where(tv[n] == n, f, nf - 1)

    grid_spec = pltpu.PrefetchScalarGridSpec(
        num_scalar_prefetch=2,
        grid=(n_tiles, nf),
        in_specs=[pl.BlockSpec((tm, d), lambda n, f, te, tv: (tv[n], 0)),
                  pl.BlockSpec((1, d), lambda n, f, te, tv: (0, 0)),
                  pl.BlockSpec((1, d, tf), lambda n, f, te, tv: (te[n], 0, f_eff(n, f, tv))),
                  pl.BlockSpec((1, d, tf), lambda n, f, te, tv: (te[n], 0, f_eff(n, f, tv))),
                  pl.BlockSpec((1, tf, d), lambda n, f, te, tv: (te[n], f_eff(n, f, tv), 0))],
        out_specs=pl.BlockSpec((tm, d), lambda n, f, te, tv: (n, 0)),
        scratch_shapes=[pltpu.VMEM((tm, d), MXU_DTYPE), pltpu.VMEM((tm, d), F32)],
    )
    return pl.pallas_call(
        _experts_kernel,
        out_shape=jax.ShapeDtypeStruct((rows, d), F32),
        grid_spec=grid_spec,
        compiler_params=_cparams(("arbitrary", "arbitrary")),
        name="moe_experts",
    )(tile_expert, tile_row, xs, g, wg, wu, wd)


def _combine_kernel(dest_ref, y_ref, wts_ref, x_ref, g_ref, o_ref, buf_ref, sem, *, tm):
    def issue(t, carry):
        _row_copy(y_ref, dest_ref[0, 0, 2 * t], buf_ref.at[0], t, sem).start()
        _row_copy(y_ref, dest_ref[0, 0, 2 * t + 1], buf_ref.at[1], t, sem).start()
        return carry

    lax.fori_loop(0, tm, issue, 0, unroll=8)

    def drain(t, carry):
        _row_copy(y_ref, 0, buf_ref.at[0], 0, sem).wait()
        _row_copy(y_ref, 0, buf_ref.at[1], 0, sem).wait()
        return carry

    lax.fori_loop(0, tm, drain, 0, unroll=8)
    w = wts_ref[...]
    y = w[:, 0:1] * buf_ref[0] + w[:, 1:2] * buf_ref[1]
    o_ref[...] = x_ref[...] + _rms(y, g_ref[...])


def _combine(y, dest, wts, x, g, tm=COMBINE_TM):
    m, d = x.shape
    row = lambda i: (i, 0)
    return pl.pallas_call(
        functools.partial(_combine_kernel, tm=tm),
        out_shape=jax.ShapeDtypeStruct((m, d), F32),
        grid=(m // tm,),
        in_specs=[pl.BlockSpec((1, 1, 2 * tm), lambda i: (i, 0, 0), memory_space=pltpu.SMEM),
                  pl.BlockSpec(memory_space=pl.ANY),
                  pl.BlockSpec((tm, LANES), row), pl.BlockSpec((tm, d), row),
                  pl.BlockSpec((1, d), lambda i: (0, 0))],
        out_specs=pl.BlockSpec((tm, d), row),
        scratch_shapes=[pltpu.VMEM((2, tm, d), F32), pltpu.SemaphoreType.DMA(())],
        compiler_params=_cparams(("arbitrary",)),
        name="moe_combine",
    )(dest.reshape(m // tm, 1, 2 * tm), y, wts, x, g)


def _rotary_tables(seq):
    pos = jnp.arange(seq, dtype=F32)
    inv = ROPE_THETA ** (-jnp.arange(0, ROT_DIM, 2, dtype=F32) / ROT_DIM)
    ang = pos[:, None] * inv[None, :]
    cos, sin = jnp.cos(ang), jnp.sin(ang)
    half = ROT_DIM // 2
    ones = jnp.ones((seq, HEAD_DIM - ROT_DIM), F32)
    zeros_h = jnp.zeros((seq, half), F32)
    zeros_t = jnp.zeros((seq, HEAD_DIM - ROT_DIM), F32)
    c = jnp.concatenate([cos, cos, ones], axis=1)
    s1 = jnp.concatenate([zeros_h, sin, zeros_t], axis=1)
    s2 = jnp.concatenate([-sin, zeros_h, zeros_t], axis=1)
    return c, s1, s2


def _pad_cols(w, n):
    return jnp.pad(w, ((0, 0), (0, n - w.shape[1])))


def _nsa_layer(x, hn, gains, w_in, pos_k, pos_v, wk1, wk2, wv1, wv2, w_out, batch, seq):
    qd, kvd = N_HEADS * HEAD_DIM, KV_GROUPS * HEAD_DIM
    cuts = [0, qd] + [qd + k * kvd for k in range(1, 7)] + [w_in.shape[1]]
    q_w, kc_w, vc_w, ksl_w, vsl_w, kw_w, vw_w, g_w = [w_in[:, a:b] for a, b in zip(cuts[:-1], cuts[1:])]
    tabs = _rotary_tables(seq)
    w_main = jnp.concatenate([q_w, ksl_w, vsl_w, kw_w, vw_w], axis=1).astype(MXU_DTYPE)
    nq_tiles = qd // PROJ_SUB
    qkv = _proj_rot(hn, w_main, tabs, seq, rot_tiles=tuple(range(nq_tiles)) + (nq_tiles, nq_tiles + 2),
                    scale_tiles=tuple(range(nq_tiles)), scale=Q_SCALE, out_dtype=MXU_DTYPE)
    w_c = jnp.concatenate([kc_w, vc_w], axis=1).astype(MXU_DTYPE)
    kcv = _proj_rot(hn, w_c, tabs, seq, rot_tiles=(0,), scale_tiles=(), scale=1.0, out_dtype=F32)
    gates = _proj_act(hn, _pad_cols(g_w, LANES).astype(MXU_DTYPE), jnp.zeros((1, LANES), F32),
                      act="sigmoid", out_dtype=F32)
    ck, cv = _compress(kcv, pos_k, pos_v, wk1.astype(MXU_DTYPE), wv1.astype(MXU_DTYPE),
                       wk2.astype(MXU_DTYPE), wv2.astype(MXU_DTYPE), batch, seq)
    attn = _nsa_attention(qkv, ck, cv, gates, batch, seq)
    return _outproj(attn, w_out.astype(MXU_DTYPE), x, gains[1:2], gains[2:3])


def _fox_layer(x, hn, gains, w_in, f_bias, w_out, batch, seq, tq=FOX_TQ):
    qkv_d = 3 * N_HEADS * HEAD_DIM
    nq_tiles = N_HEADS * HEAD_DIM // PROJ_SUB
    qkv = _proj_rot(hn, w_in[:, :qkv_d].astype(MXU_DTYPE), (), seq, rot_tiles=(),
                    scale_tiles=tuple(range(nq_tiles)), scale=Q_SCALE, out_dtype=MXU_DTYPE)
    bias = jnp.pad(f_bias.astype(F32), (0, LANES - N_HEADS)).reshape(1, LANES)
    log_f = _proj_act(hn, _pad_cols(w_in[:, qkv_d:], LANES).astype(MXU_DTYPE), bias,
                      act="log_sigmoid", out_dtype=F32)
    c_tok = _cumsum_rows(log_f, batch, seq)
    c_head = c_tok.reshape(batch, seq, LANES)[:, :, :N_HEADS].transpose(0, 2, 1).reshape(
        batch, N_HEADS, seq // tq, tq)
    attn = _fox_attention(qkv, c_tok, c_head, batch, seq, tq=tq)
    return _outproj(attn, w_out.astype(MXU_DTYPE), x, gains[1:2])


def _moe(x, gains, w_router, w_gate, w_up, w_down, tm=MOE_TM):
    m, d = x.shape
    g_in, g_out = gains[2:3], gains[3:4]
    wts, meta, counts = _router(x, g_in, _pad_cols(w_router.astype(F32), LANES))
    i1, i2, r1, r2 = meta[:, 0], meta[:, 1], meta[:, 2], meta[:, 3]
    cnt = counts[0, :N_EXPERTS].astype(jnp.int32)
    padded = ((cnt + tm - 1) // tm) * tm
    ends = jnp.cumsum(padded)
    starts = ends - padded
    dest = jnp.stack([starts[i1] + r1, starts[i2] + r2], axis=1).reshape(-1).astype(jnp.int32)
    n_tiles = (2 * m) // tm + N_EXPERTS
    tile_ids = jnp.arange(n_tiles, dtype=jnp.int32)
    last_live = jnp.maximum(ends[-1] // tm - 1, 0).astype(jnp.int32)
    tile_row = jnp.minimum(tile_ids, last_live)
    tile_expert = jnp.sum((tile_row * tm)[:, None] >= ends[None, :], axis=1).astype(jnp.int32)
    tile_expert = jnp.minimum(tile_expert, N_EXPERTS - 1)
    npad = padded - cnt
    cpad = jnp.cumsum(npad)
    k = jnp.arange(n_tiles * tm - 2 * m, dtype=jnp.int32)
    e_k = jnp.minimum(jnp.sum(k[:, None] >= cpad[None, :], axis=1), N_EXPERTS - 1)
    pad_row = (starts + cnt)[e_k] + (k - (cpad - npad)[e_k])
    fill_dst = jnp.where(k < cpad[-1], pad_row, ends[-1] + (k - cpad[-1]))
    xs = _dispatch(x, dest, fill_dst.reshape(1, -1).astype(jnp.int32), n_tiles * tm)
    y = _experts(xs, g_in, w_gate.astype(MXU_DTYPE), w_up.astype(MXU_DTYPE), w_down.astype(MXU_DTYPE),
                 tile_expert, tile_row, tm)
    return _combine(y, dest, wts, x, g_out)


def kernel(x, ln_gains, nsa_w_in, nsa_cmp_pos_k, nsa_cmp_pos_v, nsa_cmp_wk1, nsa_cmp_wk2, nsa_cmp_wv1, nsa_cmp_wv2,
           nsa_w_out, fox_w_in, fox_f_bias, fox_w_out, ffn_w_gate, ffn_w_up, ffn_w_down, moe_router, moe_w_gate,
           moe_w_up, moe_w_down):
    batch, seq, d = x.shape
    depth = ln_gains.shape[0]
    xf = x.reshape(batch * seq, d).astype(F32)
    hn = _rmsnorm(xf, ln_gains[0, 0:1])
    for i in range(depth):
        j = i // 2
        gains = ln_gains[i]
        last = i == depth - 1
        if i % 2 == 0:
            xf, hn2 = _nsa_layer(xf, hn, gains, nsa_w_in[j], nsa_cmp_pos_k[j], nsa_cmp_pos_v[j], nsa_cmp_wk1[j],
                                 nsa_cmp_wk2[j], nsa_cmp_wv1[j], nsa_cmp_wv2[j], nsa_w_out[j], batch, seq)
            g_next = ln_gains[i + 1, 0:1] if not last else gains[0:1]
            xf, hn = _ffn(hn2, ffn_w_gate[j].astype(MXU_DTYPE), ffn_w_up[j].astype(MXU_DTYPE),
                          ffn_w_down[j].astype(MXU_DTYPE), xf, gains[3:4], g_next)
        else:
            xf = _fox_layer(xf, hn, gains, fox_w_in[j], fox_f_bias[j], fox_w_out[j], batch, seq)
            xf = _moe(xf, gains, moe_router[j], moe_w_gate[j], moe_w_up[j], moe_w_down[j])
            if not last:
                hn = _rmsnorm(xf, ln_gains[i + 1, 0:1])
    return xf.reshape(batch, seq, d).astype(x.dtype)
```
